```python
import jax, jax.numpy as jnp
from jax import lax
import numpy as np

D_MODEL = 1024
BATCH = 4
SEQ = 4096
DEPTH = 4

MEM_LEN = 256
MIX_W = D_MODEL // 2
N_BRANCH = 4
CONV_W = MIX_W
CONV_K = 31
DN_HEADS = 4
DN_HEAD_DIM = MIX_W // DN_HEADS
DN_CONV_K = 4
DN_CHUNK = 64
GM_W = MIX_W
GM_GROUPS = 4
GM_CHUNK = 128
POOL_W = MIX_W
POOL_WINDOWS = (2, 4, 8, 16)
POOL_GROUPS = len(POOL_WINDOWS)
XA_HEADS = 4
XA_HEAD_DIM = D_MODEL // XA_HEADS
FFN_W = 4 * D_MODEL
IN_W = 2 * CONV_W + 4 * MIX_W + 2 * DN_HEADS + 2 * GM_W + POOL_W + N_BRANCH * D_MODEL

kernel_name = 'hybrid_gated_conv_deltanet_gmlp_pool_trunk'


def rmsnorm(x, g, eps=1e-6):
    xf = x.astype(jnp.float32)
    y = xf * lax.rsqrt(jnp.mean(xf * xf, axis=-1, keepdims=True) + eps)
    return (y * g.astype(jnp.float32)).astype(x.dtype)


def layernorm(x, g, b, eps=1e-5):
    xf = x.astype(jnp.float32)
    mu = jnp.mean(xf, axis=-1, keepdims=True)
    var = jnp.mean(jnp.square(xf - mu), axis=-1, keepdims=True)
    y = (xf - mu) * lax.rsqrt(var + eps)
    return (y * g.astype(jnp.float32) + b.astype(jnp.float32)).astype(x.dtype)


def causal_dwconv(x, w):
    K, C = w.shape
    return lax.conv_general_dilated(x, w[:, None, :].astype(x.dtype), window_strides=(1,),
                                    padding=[(K - 1, 0)],
                                    dimension_numbers=('NWC', 'WIO', 'NWC'),
                                    feature_group_count=C)


def gated_delta_rule(q, k, v, g, beta):
    B, S, H, Dk = q.shape
    Dv = v.shape[-1]
    C = DN_CHUNK
    N = S // C
    f32 = jnp.float32
    q = q.astype(f32)
    k = k.astype(f32)
    q = q * lax.rsqrt(jnp.sum(q * q, -1, keepdims=True) + 1e-6) * (Dk ** -0.5)
    k = k * lax.rsqrt(jnp.sum(k * k, -1, keepdims=True) + 1e-6)

    def chunks(t):
        t = t.astype(f32).reshape((B, N, C, H) + t.shape[3:])
        return jnp.moveaxis(t, 3, 2)

    q, k, v, g, beta = chunks(q), chunks(k), chunks(v), chunks(g), chunks(beta)
    gam = jnp.cumsum(g, axis=-1)
    causal = jnp.tril(jnp.ones((C, C), bool))
    strict = jnp.tril(jnp.ones((C, C), bool), -1)
    decay = jnp.exp(jnp.where(causal, gam[..., :, None] - gam[..., None, :], -jnp.inf))
    kb = k * beta[..., None]
    a = jnp.where(strict, jnp.einsum('bnhid,bnhjd->bnhij', kb, k) * decay, 0.0)
    rhs = jnp.concatenate([v * beta[..., None], kb * jnp.exp(gam)[..., None]], axis=-1)
    sol = lax.linalg.triangular_solve(a + jnp.eye(C, dtype=f32), rhs, left_side=True,
                                      lower=True, unit_diagonal=True)
    u, w = sol[..., :Dv], sol[..., Dv:]
    attn = jnp.einsum('bnhid,bnhjd->bnhij', q, k) * decay
    q_dec = q * jnp.exp(gam)[..., None]
    g_last = gam[..., -1]
    k_dec = k * jnp.exp(g_last[..., None] - gam)[..., None]

    def step(state, xs):
        u_n, w_n, attn_n, qd_n, kd_n, gl_n = xs
        v_new = u_n - jnp.einsum('bhck,bhkv->bhcv', w_n, state)
        o_n = (jnp.einsum('bhck,bhkv->bhcv', qd_n, state)
               + jnp.einsum('bhij,bhjv->bhiv', attn_n, v_new))
        state = (state * jnp.exp(gl_n)[..., None, None]
                 + jnp.einsum('bhck,bhcv->bhkv', kd_n, v_new))
        return state, o_n

    xs = tuple(jnp.moveaxis(t, 1, 0) for t in (u, w, attn, q_dec, k_dec, g_last))
    state0 = jnp.zeros((B, H, Dk, Dv), f32)
    _, o = lax.scan(step, state0, xs)
    return jnp.transpose(o, (1, 0, 3, 2, 4)).reshape(B, S, H, Dv)


def multiscale_pool(x):
    B, S, C = x.shape
    Cg = C // POOL_GROUPS
    xf = x.astype(jnp.float32).reshape(B, S, POOL_GROUPS, Cg)
    cs = jnp.cumsum(xf, axis=1)
    cs_pad = jnp.concatenate([jnp.zeros((B, 1, POOL_GROUPS, Cg), jnp.float32), cs], axis=1)
    t = jnp.arange(S)
    outs = []
    for gi, win in enumerate(POOL_WINDOWS):
        upper = cs[:, :, gi]
        lower = jnp.concatenate([jnp.zeros((B, win - 1, Cg), jnp.float32),
                                 cs_pad[:, :S - win + 1, gi]], axis=1)
        count = jnp.minimum(t + 1, win).astype(jnp.float32)[None, :, None]
        outs.append((upper - lower) / count)
    return (jnp.stack(outs, axis=2) - xf).astype(x.dtype)


def hybrid_mixer(h, w_in, conv_a_w, conv_a_b, ln_a_g, ln_a_b, dn_conv_w, dn_a_log, dn_dt_bias,
                 dn_norm_g, gm_ln_g, gm_ln_b, gm_ws, gm_bs, pool_w, pool_scale, w_branch, w_out):
    B, S, _ = h.shape
    f32 = jnp.float32
    z = h @ w_in
    sizes = [2 * CONV_W, MIX_W, MIX_W, MIX_W, MIX_W, DN_HEADS, DN_HEADS, 2 * GM_W, POOL_W,
             N_BRANCH * D_MODEL]
    idx = np.cumsum(sizes)[:-1].tolist()
    a_in, dq, dk, dv, dgate, dbeta, da, gm_in, pool_in, gate_in = jnp.split(z, idx, axis=-1)

    a1, a2 = jnp.split(a_in, 2, axis=-1)
    a = a1 * jax.nn.sigmoid(a2)
    a = causal_dwconv(a, conv_a_w) + conv_a_b
    a = jax.nn.silu(layernorm(a, ln_a_g, ln_a_b))

    qkv = jax.nn.silu(causal_dwconv(jnp.concatenate([dq, dk, dv], axis=-1), dn_conv_w))
    q, k, v = jnp.split(qkv, 3, axis=-1)
    heads = lambda t: t.reshape(B, S, DN_HEADS, DN_HEAD_DIM)
    beta = jax.nn.sigmoid(dbeta.astype(f32))
    g = -jnp.exp(dn_a_log.astype(f32)) * jax.nn.softplus(da.astype(f32) + dn_dt_bias.astype(f32))
    o = gated_delta_rule(heads(q), heads(k), heads(v), g, beta)
    o = rmsnorm(o, dn_norm_g) * jax.nn.silu(heads(dgate).astype(f32))
    o = o.reshape(B, S, MIX_W).astype(h.dtype)

    u, vg = jnp.split(jax.nn.gelu(gm_in, approximate=False), 2, axis=-1)
    vg = layernorm(vg, gm_ln_g, gm_ln_b)
    vc = vg.reshape(B, S // GM_CHUNK, GM_CHUNK, GM_GROUPS, GM_W // GM_GROUPS)
    ws = gm_ws * jnp.tril(jnp.ones((GM_CHUNK, GM_CHUNK), gm_ws.dtype))
    mixed = jnp.einsum('gts,bnsgc->bntgc', ws, vc) + gm_bs.T[:, :, None]
    c = u * mixed.reshape(B, S, GM_W)

    p = multiscale_pool(pool_in)
    p = jnp.einsum('bsgc,gcd->bsgd', p, pool_w).reshape(B, S, POOL_W) * pool_scale

    gates = jax.nn.sigmoid(gate_in).reshape(B, S, N_BRANCH, D_MODEL)
    br = jnp.stack([a, o, c, p], axis=2)
    proj = jnp.einsum('bsnc,ncd->bsnd', br, w_branch)
    merged = jnp.sum(gates * proj, axis=2)
    return merged @ w_out


def memory_cross_attention(h, mem, norm_mem, wq, wkv, wo):
    B, S, _ = h.shape
    M = mem.shape[1]
    m = rmsnorm(mem, norm_mem)
    q = (h @ wq).reshape(B, S, XA_HEADS, XA_HEAD_DIM)
    k, v = jnp.split(m @ wkv, 2, axis=-1)
    k = k.reshape(B, M, XA_HEADS, XA_HEAD_DIM)
    v = v.reshape(B, M, XA_HEADS, XA_HEAD_DIM)
    s = jnp.einsum('bshd,bmhd->bhsm', q, k).astype(jnp.float32) * (XA_HEAD_DIM ** -0.5)
    pr = jax.nn.softmax(s, axis=-1).astype(v.dtype)
    o = jnp.einsum('bhsm,bmhd->bshd', pr, v).reshape(B, S, D_MODEL)
    return o @ wo


def squared_relu_mlp(h, w1, w2):
    return jnp.square(jax.nn.relu(h @ w1)) @ w2


def setup_inputs(seed: int = 0) -> dict:
    key = jax.random.key(seed)
    ks = jax.random.split(key, 40)
    f32 = jnp.float32
    L, D = DEPTH, D_MODEL
    nrm = lambda i, shape, scale: jax.random.normal(ks[i], shape, f32) * scale
    gain = lambda i, shape: 1.0 + 0.1 * jax.random.normal(ks[i], shape, f32)
    dt = jnp.exp(jax.random.uniform(ks[10], (L, DN_HEADS), f32) * (np.log(0.1) - np.log(0.001))
                 + np.log(0.001))
    return {
        'x': nrm(0, (BATCH, SEQ, D), 1.0),
        'mem': nrm(1, (BATCH, MEM_LEN, D), 1.0),
        'norm_mix': gain(2, (L, D)),
        'w_in': nrm(3, (L, D, IN_W), D ** -0.5),
        'conv_a_w': nrm(4, (L, CONV_K, CONV_W), CONV_K ** -0.5),
        'conv_a_b': nrm(5, (L, CONV_W), 0.02),
        'ln_a_g': gain(6, (L, CONV_W)),
        'ln_a_b': nrm(7, (L, CONV_W), 0.02),
        'dn_conv_w': nrm(8, (L, DN_CONV_K, 3 * MIX_W), DN_CONV_K ** -0.5),
        'dn_a_log': jnp.log(jax.random.uniform(ks[9], (L, DN_HEADS), f32, 1.0, 16.0)),
        'dn_dt_bias': dt + jnp.log(-jnp.expm1(-dt)),
        'dn_norm_g': gain(11, (L, DN_HEAD_DIM)),
        'gm_ln_g': gain(12, (L, GM_W)),
        'gm_ln_b': nrm(13, (L, GM_W), 0.02),
        'gm_ws': nrm(14, (L, GM_GROUPS, GM_CHUNK, GM_CHUNK), GM_CHUNK ** -0.5),
        'gm_bs': gain(15, (L, GM_GROUPS, GM_CHUNK)),
        'pool_w': nrm(16, (L, POOL_GROUPS, POOL_W // POOL_GROUPS, POOL_W // POOL_GROUPS),
                      (POOL_W // POOL_GROUPS) ** -0.5),
        'pool_scale': gain(17, (L, POOL_W)),
        'w_branch': nrm(18, (L, N_BRANCH, MIX_W, D), MIX_W ** -0.5),
        'w_out': nrm(19, (L, D, D), D ** -0.5),
        'norm_xa': gain(20, (L, D)),
        'norm_mem': gain(21, (L, D)),
        'xa_wq': nrm(22, (L, D, D), D ** -0.5),
        'xa_wkv': nrm(23, (L, D, 2 * D), D ** -0.5),
        'xa_wo': nrm(24, (L, D, D), D ** -0.5),
        'norm_mlp': gain(25, (L, D)),
        'mlp_w1': nrm(26, (L, D, FFN_W), D ** -0.5),
        'mlp_w2': nrm(27, (L, FFN_W, D), FFN_W ** -0.5),
        'norm_f': gain(28, (D,)),
    }


def reference(x, mem, norm_mix, w_in, conv_a_w, conv_a_b, ln_a_g, ln_a_b, dn_conv_w, dn_a_log,
              dn_dt_bias, dn_norm_g, gm_ln_g, gm_ln_b, gm_ws, gm_bs, pool_w, pool_scale, w_branch,
              w_out, norm_xa, norm_mem, xa_wq, xa_wkv, xa_wo, norm_mlp, mlp_w1, mlp_w2, norm_f):
    for l in range(DEPTH):
        x = x + hybrid_mixer(rmsnorm(x, norm_mix[l]), w_in[l], conv_a_w[l], conv_a_b[l], ln_a_g[l],
                             ln_a_b[l], dn_conv_w[l], dn_a_log[l], dn_dt_bias[l], dn_norm_g[l],
                             gm_ln_g[l], gm_ln_b[l], gm_ws[l], gm_bs[l], pool_w[l], pool_scale[l],
                             w_branch[l], w_out[l])
        x = x + memory_cross_attention(rmsnorm(x, norm_xa[l]), mem, norm_mem[l], xa_wq[l],
                                       xa_wkv[l], xa_wo[l])
        x = x + squared_relu_mlp(rmsnorm(x, norm_mlp[l]), mlp_w1[l], mlp_w2[l])
    return rmsnorm(x, norm_f)
```

```python
import functools

import jax
import jax.numpy as jnp
import numpy as np
from jax import lax
from jax.experimental import pallas as pl
from jax.experimental.pallas import tpu as pltpu

F32 = jnp.float32
BF16 = jnp.bfloat16

D_MODEL = 1024
MIX_W = D_MODEL // 2
N_BRANCH = 4
CONV_K = 31
DN_HEADS = 4
DN_HEAD_DIM = MIX_W // DN_HEADS
DN_CONV_K = 4
DN_CHUNK = 64
GM_GROUPS = 4
GM_CHUNK = 128
POOL_WINDOWS = (2, 4, 8, 16)
XA_HEADS = 4
XA_HEAD_DIM = D_MODEL // XA_HEADS
FFN_W = 4 * D_MODEL

V7X_VMEM_BYTES = 64 * 1024 * 1024
LANES = 128
SUBLANES = 8

Z_W = 2 * MIX_W + 2 * MIX_W + 4 * MIX_W + N_BRANCH * D_MODEL + MIX_W
Z_BN = 512
SMALL_W = 2 * LANES

CONV_HALO = 32
POOL_HALO = 32
DN_HALO = 8
LOCAL_TS = 512
DN_TS = 256


def _vmem_limit(block_bytes, scratch_bytes=0, temp_bytes=0):
    need = 2 * block_bytes + scratch_bytes + temp_bytes + (4 << 20)
    return int(min(need, V7X_VMEM_BYTES - (6 << 20)))


def _nbytes(shape, dtype):
    return int(np.prod(shape)) * jnp.dtype(dtype).itemsize


def _rmsnorm(x, g, eps=1e-6):
    return x * lax.rsqrt(jnp.mean(x * x, axis=-1, keepdims=True) + eps) * g


def _layernorm(x, g, b, eps=1e-5):
    mu = jnp.mean(x, axis=-1, keepdims=True)
    xc = x - mu
    var = jnp.mean(xc * xc, axis=-1, keepdims=True)
    return xc * lax.rsqrt(var + eps) * g + b


def _sigmoid(x):
    return 1.0 / (1.0 + jnp.exp(-x))


def _dot(a, b):
    return jnp.dot(a, b, preferred_element_type=F32)


def _dot_nt(a, b):
    return lax.dot_general(a, b, (((1,), (1,)), ((), ())), preferred_element_type=F32)


def _dot_tn(a, b):
    return lax.dot_general(a, b, (((0,), (0,)), ((), ())), preferred_element_type=F32)


def _dot_f32(a, b):
    return jnp.dot(a, b, preferred_element_type=F32, precision=lax.Precision.HIGHEST)


def _inproj_kernel(x_ref, g_ref, w_ref, ws_ref, z_ref, sm_ref, h_ref):
    @pl.when(pl.program_id(1) == 0)
    def _():
        h = _rmsnorm(x_ref[...], g_ref[...]).astype(BF16)
        h_ref[...] = h
        sm_ref[...] = _dot(h, ws_ref[...])

    z_ref[...] = _dot(h_ref[...], w_ref[...]).astype(z_ref.dtype)


def _inproj(x, g, w, ws):
    T, D = x.shape
    bm = min(1024, T)
    nj = Z_W // Z_BN
    blocks = (_nbytes((bm, D), F32) + _nbytes((D, Z_BN), BF16) + _nbytes((D, SMALL_W), BF16)
              + _nbytes((bm, Z_BN), BF16) + _nbytes((bm, SMALL_W), F32))
    return pl.pallas_call(
        _inproj_kernel,
        grid=(T // bm, nj),
        in_specs=[
            pl.BlockSpec((bm, D), lambda i, j: (i, 0)),
            pl.BlockSpec((1, D), lambda i, j: (0, 0)),
            pl.BlockSpec((D, Z_BN), lambda i, j: (0, j)),
            pl.BlockSpec((D, SMALL_W), lambda i, j: (0, 0)),
        ],
        out_specs=[
            pl.BlockSpec((bm, Z_BN), lambda i, j: (i, j)),
            pl.BlockSpec((bm, SMALL_W), lambda i, j: (i, 0)),
        ],
        out_shape=[jax.ShapeDtypeStruct((T, Z_W), BF16), jax.ShapeDtypeStruct((T, SMALL_W), F32)],
        scratch_shapes=[pltpu.VMEM((bm, D), BF16)],
        compiler_params=pltpu.CompilerParams(
            dimension_semantics=("parallel", "arbitrary"),
            vmem_limit_bytes=_vmem_limit(blocks, _nbytes((bm, D), BF16), _nbytes((bm, D), F32))),
        name="inproj",
    )(x, g, w, ws)


def _rms_matmul_kernel(x_ref, g_ref, w_ref, o_ref, h_ref):
    @pl.when(pl.program_id(1) == 0)
    def _():
        h_ref[...] = _rmsnorm(x_ref[...], g_ref[...]).astype(BF16)

    o_ref[...] = _dot(h_ref[...], w_ref[...]).astype(o_ref.dtype)


def _rms_matmul(x, g, w, bn):
    T, D = x.shape
    N = w.shape[1]
    bm = min(1024, T)
    blocks = _nbytes((bm, D), F32) + _nbytes((D, bn), BF16) + _nbytes((bm, bn), BF16)
    return pl.pallas_call(
        _rms_matmul_kernel,
        grid=(T // bm, N // bn),
        in_specs=[
            pl.BlockSpec((bm, D), lambda i, j: (i, 0)),
            pl.BlockSpec((1, D), lambda i, j: (0, 0)),
            pl.BlockSpec((D, bn), lambda i, j: (0, j)),
        ],
        out_specs=pl.BlockSpec((bm, bn), lambda i, j: (i, j)),
        out_shape=jax.ShapeDtypeStruct((T, N), BF16),
        scratch_shapes=[pltpu.VMEM((bm, D), BF16)],
        compiler_params=pltpu.CompilerParams(
            dimension_semantics=("parallel", "arbitrary"),
            vmem_limit_bytes=_vmem_limit(blocks, _nbytes((bm, D), BF16), _nbytes((bm, D), F32))),
        name="mem_kv",
    )(x, g, w)


def _local_kernel(ain_ref, gm_ref, pin_ref, cw_ref, cb_ref, lag_ref, lab_ref, glg_ref, glb_ref,
                  ws_ref, bs_ref, pw_ref, ps_ref, out_ref, abuf, ybuf, pbuf, tbuf):
    s = pl.program_id(1)
    ts = ain_ref.shape[0]
    w = MIX_W

    @pl.when(s == 0)
    def _():
        abuf[0:CONV_HALO, :] = jnp.zeros((CONV_HALO, w), F32)
        pbuf[0:POOL_HALO, :] = jnp.zeros((POOL_HALO, w), F32)

    ain = ain_ref[...].astype(F32)
    abuf[CONV_HALO:CONV_HALO + ts, :] = ain[:, :w] * _sigmoid(ain[:, w:])
    rb = 32
    for r0 in range(0, ts, rb):
        acc = jnp.broadcast_to(cb_ref[...], (rb, w))
        for k in range(CONV_K):
            off = CONV_HALO - (CONV_K - 1) + k + r0
            acc = acc + cw_ref[k:k + 1, :] * abuf[off:off + rb, :]
        ybuf[r0:r0 + rb, :] = acc
    abuf[0:CONV_HALO, :] = abuf[ts:ts + CONV_HALO, :]
    ya = _layernorm(ybuf[...], lag_ref[...], lab_ref[...])
    out_ref[:, 0:w] = (ya * _sigmoid(ya)).astype(out_ref.dtype)

    gm = gm_ref[...].astype(F32)
    ge = 0.5 * gm * (1.0 + lax.erf(gm * np.float32(np.sqrt(0.5))))
    ybuf[...] = _layernorm(ge[:, w:], glg_ref[...], glb_ref[...])
    gw = w // GM_GROUPS
    row = lax.broadcasted_iota(jnp.int32, (GM_CHUNK, GM_CHUNK), 0)
    col = lax.broadcasted_iota(jnp.int32, (GM_CHUNK, GM_CHUNK), 1)
    for g in range(GM_GROUPS):
        wsg = jnp.where(row >= col, ws_ref[g], 0.0).astype(BF16)
        for n in range(ts // GM_CHUNK):
            rows = slice(n * GM_CHUNK, (n + 1) * GM_CHUNK)
            cols = slice(g * gw, (g + 1) * gw)
            mixed = _dot(wsg, ybuf[rows, cols].astype(BF16)) + bs_ref[:, cols]
            out_ref[rows, w + g * gw:w + (g + 1) * gw] = (ge[rows, cols] * mixed).astype(out_ref.dtype)

    pin = pin_ref[...].astype(F32)
    pbuf[POOL_HALO:POOL_HALO + ts, :] = pin
    tpos = s * ts + lax.broadcasted_iota(jnp.int32, (ts, 1), 0)
    n_ext = POOL_HALO + ts
    pw_cols = w // len(POOL_WINDOWS)
    for gi, win in enumerate(POOL_WINDOWS):
        cols = slice(gi * pw_cols, (gi + 1) * pw_cols)
        src = pbuf
        d = 1
        lo = SUBLANES
        while d < win:
            tbuf[lo:n_ext, cols] = src[lo:n_ext, cols] + src[lo - d:n_ext - d, cols]
            src = tbuf
            d *= 2
            lo += SUBLANES
        wsum = src[POOL_HALO:n_ext, cols]
        inv = 1.0 / jnp.minimum(tpos + 1, win).astype(F32)
        pg = (wsum * inv - pin[:, cols]).astype(BF16)
        lin = _dot(pg, pw_ref[gi]) * ps_ref[:, cols]
        out_ref[:, 2 * w + gi * pw_cols:2 * w + (gi + 1) * pw_cols] = lin.astype(out_ref.dtype)
    pbuf[0:POOL_HALO, :] = pbuf[ts:ts + POOL_HALO, :]


def _local_mixers(z, B, S, cw, cb, lag, lab, glg, glb, ws, bs_full, pw, ps):
    ts = min(LOCAL_TS, S)
    ns = S // ts
    w = MIX_W
    const = lambda *shape: pl.BlockSpec(shape, lambda b, s: (0,) * len(shape))
    blocks = (_nbytes((ts, 2 * w), BF16) * 2 + _nbytes((ts, w), BF16) + _nbytes((ts, 3 * w), BF16)
              + _nbytes((CONV_K + 8, w), F32) + _nbytes((GM_GROUPS, GM_CHUNK, GM_CHUNK), F32)
              + _nbytes((GM_CHUNK, w), F32) + _nbytes((4, 128, 128), BF16))
    scratch = (_nbytes((CONV_HALO + ts, w), F32) + _nbytes((ts, w), F32)
               + 2 * _nbytes((POOL_HALO + ts, w), F32))
    return pl.pallas_call(
        _local_kernel,
        grid=(B, ns),
        in_specs=[
            pl.BlockSpec((ts, 2 * w), lambda b, s: (b * ns + s, 0)),
            pl.BlockSpec((ts, 2 * w), lambda b, s: (b * ns + s, 1)),
            pl.BlockSpec((ts, w), lambda b, s: (b * ns + s, (Z_W - w) // w)),
            const(CONV_K, w), const(1, w), const(1, w), const(1, w), const(1, w), const(1, w),
            const(GM_GROUPS, GM_CHUNK, GM_CHUNK), const(GM_CHUNK, w),
            const(len(POOL_WINDOWS), w // 4, w // 4), const(1, w),
        ],
        out_specs=pl.BlockSpec((ts, 3 * w), lambda b, s: (b * ns + s, 0)),
        out_shape=jax.ShapeDtypeStruct((B * S, 3 * w), BF16),
        scratch_shapes=[
            pltpu.VMEM((CONV_HALO + ts, w), F32),
            pltpu.VMEM((ts, w), F32),
            pltpu.VMEM((POOL_HALO + ts, w), F32),
            pltpu.VMEM((POOL_HALO + ts, w), F32),
        ],
        compiler_params=pltpu.CompilerParams(
            dimension_semantics=("parallel", "arbitrary"),
            vmem_limit_bytes=_vmem_limit(blocks, scratch, 8 * _nbytes((ts, 2 * w), F32))),
        name="local_mixers",
    )(z, z, z, cw, cb, lag, lab, glg, glb, ws, bs_full, pw, ps)


def _split3(x):
    hi = x.astype(BF16)
    r1 = x - hi.astype(F32)
    mid = r1.astype(BF16)
    lo = (r1 - mid.astype(F32)).astype(BF16)
    return hi, mid, lo


def _dot_mask(mask_bf16, x):
    hi, mid, lo = _split3(x)
    return _dot(mask_bf16, hi) + _dot(mask_bf16, mid) + _dot(mask_bf16, lo)


def _deltanet_kernel(qk_ref, vd_ref, sm_ref, cw_ref, alog_ref, dtb_ref, ng_ref, o_ref,
                     xbuf, state, sol_s, vnew_s, qs_s):
    s = pl.program_id(1)
    ts = qk_ref.shape[0]
    w = MIX_W
    dh = DN_HEAD_DIM
    ck = DN_CHUNK

    @pl.when(s == 0)
    def _():
        xbuf[0:DN_HALO, :] = jnp.zeros((DN_HALO, 3 * w), F32)
        state[...] = jnp.zeros(state.shape, F32)

    xbuf[DN_HALO:DN_HALO + ts, 0:2 * w] = qk_ref[...].astype(F32)
    xbuf[DN_HALO:DN_HALO + ts, 2 * w:3 * w] = vd_ref[:, 0:w].astype(F32)
    y = jnp.zeros((ts, 3 * w), F32)
    for k in range(DN_CONV_K):
        off = DN_HALO - (DN_CONV_K - 1) + k
        y = y + cw_ref[k:k + 1, :] * xbuf[off:off + ts, :]
    xbuf[0:DN_HALO, :] = xbuf[ts:ts + DN_HALO, :]
    qkv = y * _sigmoid(y)

    sm = sm_ref[...]
    beta = _sigmoid(sm[:, 0:LANES])
    da = sm[:, LANES:2 * LANES] + dtb_ref[...]
    softplus = jnp.maximum(da, 0.0) + jnp.log1p(jnp.exp(-jnp.abs(da)))
    g = -jnp.exp(alog_ref[...]) * softplus

    row = lax.broadcasted_iota(jnp.int32, (ts, ts), 0)
    col = lax.broadcasted_iota(jnp.int32, (ts, ts), 1)
    same_chunk = (row // ck) == (col // ck)
    causal = same_chunk & (row >= col)
    strict = same_chunk & (row > col)
    gam = _dot_mask(causal.astype(BF16), g)
    glast = _dot_mask(same_chunk.astype(BF16), g)
    gam_t = gam.T
    eye = (row == col).astype(F32)

    for h in range(DN_HEADS):
        hs = slice(h * dh, (h + 1) * dh)
        qh = qkv[:, h * dh:(h + 1) * dh]
        kh = qkv[:, w + h * dh:w + (h + 1) * dh]
        vh = qkv[:, 2 * w + h * dh:2 * w + (h + 1) * dh]
        qn = qh * lax.rsqrt(jnp.sum(qh * qh, axis=-1, keepdims=True) + 1e-6) * np.float32(dh ** -0.5)
        kn = kh * lax.rsqrt(jnp.sum(kh * kh, axis=-1, keepdims=True) + 1e-6)
        bcol = beta[:, h:h + 1]
        gcol = gam[:, h:h + 1]
        grow = gam_t[h:h + 1, :]
        glcol = glast[:, h:h + 1]

        kb = kn * bcol
        kn16 = kn.astype(BF16)
        decay = jnp.where(causal, jnp.exp(jnp.where(causal, gcol - grow, 0.0)), 0.0)
        a = jnp.where(strict, _dot_nt(kb.astype(BF16), kn16) * decay, 0.0)
        attn = _dot_nt(qn.astype(BF16), kn16) * decay

        x = eye - jnp.where(same_chunk & ((row ^ col) == 1) & (row > col), a, 0.0)
        m = 2
        while m < ck:
            low_left = ((row // (2 * m)) == (col // (2 * m))) & ((row & m) != 0) & ((col & m) == 0)
            bm = jnp.where(low_left, a, 0.0)
            x = x - _dot_f32(x, _dot_f32(bm, x))
            m *= 2

        eg = jnp.exp(gcol)
        rhs = jnp.concatenate([vh * bcol, kb * eg], axis=1)
        sol_s[...] = _dot_f32(x, rhs)
        qdec = (qn * eg).astype(BF16)
        kdec = (kn * jnp.exp(glcol - gcol)).astype(BF16)

        st = state[h]
        for c in range(ts // ck):
            rows = slice(c * ck, (c + 1) * ck)
            wq = jnp.concatenate([sol_s[rows, dh:2 * dh].astype(BF16), qdec[rows, :]], axis=0)
            r = _dot(wq, st.astype(BF16))
            vnew = sol_s[rows, 0:dh] - r[0:ck, :]
            vnew_s[rows, :] = vnew
            qs_s[rows, :] = r[ck:2 * ck, :]
            scale = jnp.exp(glast[c * ck:c * ck + 1, h:h + 1])
            st = st * scale + _dot_tn(kdec[rows, :], vnew.astype(BF16))
        state[h] = st

        o = qs_s[...] + _dot(attn.astype(BF16), vnew_s[...].astype(BF16))
        on = _rmsnorm(o, ng_ref[...])
        dg = vd_ref[:, w + h * dh:w + (h + 1) * dh].astype(F32)
        o_ref[:, hs] = (on * (dg * _sigmoid(dg))).astype(o_ref.dtype)


def _deltanet(z, sm, B, S, cw, alog, dtb, ng):
    ts = min(DN_TS, S)
    ns = S // ts
    w = MIX_W
    const = lambda *shape: pl.BlockSpec(shape, lambda b, s: (0,) * len(shape))
    blocks = (2 * _nbytes((ts, 2 * w), BF16) + _nbytes((ts, SMALL_W), F32) + _nbytes((8, 3 * w), F32)
              + _nbytes((ts, w), BF16))
    scratch = (_nbytes((DN_HALO + ts, 3 * w), F32) + _nbytes((DN_HEADS, DN_HEAD_DIM, DN_HEAD_DIM), F32)
               + _nbytes((ts, 4 * DN_HEAD_DIM), F32))
    return pl.pallas_call(
        _deltanet_kernel,
        grid=(B, ns),
        in_specs=[
            pl.BlockSpec((ts, 2 * w), lambda b, s: (b * ns + s, 2)),
            pl.BlockSpec((ts, 2 * w), lambda b, s: (b * ns + s, 3)),
            pl.BlockSpec((ts, SMALL_W), lambda b, s: (b * ns + s, 0)),
            const(DN_CONV_K, 3 * w), const(1, LANES), const(1, LANES), const(1, DN_HEAD_DIM),
        ],
        out_specs=pl.BlockSpec((ts, w), lambda b, s: (b * ns + s, 0)),
        out_shape=jax.ShapeDtypeStruct((B * S, w), BF16),
        scratch_shapes=[
            pltpu.VMEM((DN_HALO + ts, 3 * w), F32),
            pltpu.VMEM((DN_HEADS, DN_HEAD_DIM, DN_HEAD_DIM), F32),
            pltpu.VMEM((ts, 2 * DN_HEAD_DIM), F32),
            pltpu.VMEM((ts, DN_HEAD_DIM), F32),
            pltpu.VMEM((ts, DN_HEAD_DIM), F32),
        ],
        compiler_params=pltpu.CompilerParams(
            dimension_semantics=("parallel", "arbitrary"),
            vmem_limit_bytes=_vmem_limit(blocks, scratch, 24 * _nbytes((ts, ts), F32)
                                         + 4 * _nbytes((ts, 3 * w), F32))),
        name="deltanet",
    )(z, z, sm, cw, alog, dtb, ng)


def _merge_kernel(x_ref, acp_ref, dn_ref, g0_ref, g1_ref, g2_ref, g3_ref, wb_ref, wo_ref, out_ref):
    w = MIX_W
    branches = (acp_ref[:, 0:w], dn_ref[...], acp_ref[:, w:2 * w], acp_ref[:, 2 * w:3 * w])
    gates = (g0_ref, g1_ref, g2_ref, g3_ref)
    merged = None
    for n in range(N_BRANCH):
        term = _sigmoid(gates[n][...].astype(F32)) * _dot(branches[n], wb_ref[n])
        merged = term if merged is None else merged + term
    out_ref[...] = x_ref[...] + _dot(merged.astype(BF16), wo_ref[...])


def _merge(x, acp, dn, z, wb, wo):
    T, D = x.shape
    w = MIX_W
    bm = min(512, T)
    gate_blk0 = (4 * 2 * w) // D
    gate_spec = lambda n: pl.BlockSpec((bm, D), lambda i: (i, gate_blk0 + n))
    blocks = (2 * _nbytes((bm, D), F32) + _nbytes((bm, 4 * w), BF16) + 4 * _nbytes((bm, D), BF16)
              + _nbytes((N_BRANCH, w, D), BF16) + _nbytes((D, D), BF16))
    return pl.pallas_call(
        _merge_kernel,
        grid=(T // bm,),
        in_specs=[
            pl.BlockSpec((bm, D), lambda i: (i, 0)),
            pl.BlockSpec((bm, 3 * w), lambda i: (i, 0)),
            pl.BlockSpec((bm, w), lambda i: (i, 0)),
            gate_spec(0), gate_spec(1), gate_spec(2), gate_spec(3),
            pl.BlockSpec((N_BRANCH, w, D), lambda i: (0, 0, 0)),
            pl.BlockSpec((D, D), lambda i: (0, 0)),
        ],
        out_specs=pl.BlockSpec((bm, D), lambda i: (i, 0)),
        out_shape=jax.ShapeDtypeStruct((T, D), F32),
        compiler_params=pltpu.CompilerParams(
            dimension_semantics=("parallel",),
            vmem_limit_bytes=_vmem_limit(blocks, 0, 4 * _nbytes((bm, D), F32))),
        name="merge",
    )(x, acp, dn, z, z, z, z, wb, wo)


def _xattn_kernel(x_ref, g_ref, wq_ref, kv_ref, wo_ref, out_ref):
    x = x_ref[...]
    q = _dot(_rmsnorm(x, g_ref[...]).astype(BF16), wq_ref[...])
    hd = XA_HEAD_DIM
    outs = []
    for h in range(XA_HEADS):
        qh = q[:, h * hd:(h + 1) * hd].astype(BF16)
        kh = kv_ref[:, h * hd:(h + 1) * hd]
        vh = kv_ref[:, D_MODEL + h * hd:D_MODEL + (h + 1) * hd]
        sc = _dot_nt(qh, kh) * np.float32(hd ** -0.5)
        p = jnp.exp(sc - jnp.max(sc, axis=-1, keepdims=True))
        p = p / jnp.sum(p, axis=-1, keepdims=True)
        outs.append(_dot(p.astype(BF16), vh).astype(BF16))
    out_ref[...] = x + _dot(jnp.concatenate(outs, axis=1), wo_ref[...])


def _xattn(x, g, wq, kv, wo, S):
    T, D = x.shape
    M = kv.shape[0] // (T // S)
    bm = min(512, S)
    per_b = S // bm
    blocks = (2 * _nbytes((bm, D), F32) + 2 * _nbytes((D, D), BF16) + _nbytes((M, 2 * D), BF16))
    return pl.pallas_call(
        _xattn_kernel,
        grid=(T // bm,),
        in_specs=[
            pl.BlockSpec((bm, D), lambda i: (i, 0)),
            pl.BlockSpec((1, D), lambda i: (0, 0)),
            pl.BlockSpec((D, D), lambda i: (0, 0)),
            pl.BlockSpec((M, 2 * D), lambda i: (i // per_b, 0)),
            pl.BlockSpec((D, D), lambda i: (0, 0)),
        ],
        out_specs=pl.BlockSpec((bm, D), lambda i: (i, 0)),
        out_shape=jax.ShapeDtypeStruct((T, D), F32),
        compiler_params=pltpu.CompilerParams(
            dimension_semantics=("parallel",),
            vmem_limit_bytes=_vmem_limit(blocks, 0, 6 * _nbytes((bm, D), F32))),
        name="xattn",
    )(x, g, wq, kv, wo)


def _mlp_kernel(x_ref, g_ref, w1_ref, w2_ref, gf_ref, out_ref, h_ref, *, final_norm):
    f = pl.program_id(1)

    @pl.when(f == 0)
    def _():
        x = x_ref[...]
        h_ref[...] = _rmsnorm(x, g_ref[...]).astype(BF16)
        out_ref[...] = x

    t = jnp.maximum(_dot(h_ref[...], w1_ref[...]), 0.0)
    out_ref[...] += _dot((t * t).astype(BF16), w2_ref[...])

    if final_norm:
        @pl.when(f == pl.num_programs(1) - 1)
        def _():
            out_ref[...] = _rmsnorm(out_ref[...], gf_ref[...])


def _mlp(x, g, w1, w2, gf, final_norm):
    T, D = x.shape
    F = w1.shape[1]
    bm = min(1024, T)
    bf = 512
    blocks = (2 * _nbytes((bm, D), F32) + 2 * _nbytes((D, bf), BF16))
    return pl.pallas_call(
        functools.partial(_mlp_kernel, final_norm=final_norm),
        grid=(T // bm, F // bf),
        in_specs=[
            pl.BlockSpec((bm, D), lambda i, f: (i, 0)),
            pl.BlockSpec((1, D), lambda i, f: (0, 0)),
            pl.BlockSpec((D, bf), lambda i, f: (0, f)),
            pl.BlockSpec((bf, D), lambda i, f: (f, 0)),
            pl.BlockSpec((1, D), lambda i, f: (0, 0)),
        ],
        out_specs=pl.BlockSpec((bm, D), lambda i, f: (i, 0)),
        out_shape=jax.ShapeDtypeStruct((T, D), F32),
        scratch_shapes=[pltpu.VMEM((bm, D), BF16)],
        compiler_params=pltpu.CompilerParams(
            dimension_semantics=("parallel", "arbitrary"),
            vmem_limit_bytes=_vmem_limit(blocks, _nbytes((bm, D), BF16), 3 * _nbytes((bm, bf), F32))),
        name="mlp",
    )(x, g, w1, w2, gf)


def _permute_w_in(w_in):
    w = MIX_W
    sizes = [2 * w, w, w, w, w, DN_HEADS, DN_HEADS, 2 * w, w, N_BRANCH * D_MODEL]
    idx = np.cumsum([0] + sizes)
    seg = lambda i: w_in[:, idx[i]:idx[i + 1]]
    a_in, dq, dk, dv, dgate, dbeta, da, gm_in, pool_in, gate_in = (seg(i) for i in range(10))
    main = jnp.concatenate([a_in, gm_in, dq, dk, dv, dgate, gate_in, pool_in], axis=1).astype(BF16)
    pad = jnp.zeros((w_in.shape[0], LANES - DN_HEADS), w_in.dtype)
    small = jnp.concatenate([dbeta, pad, da, pad], axis=1).astype(BF16)
    return main, small


def _pad_lanes(v):
    return jnp.pad(v.astype(F32), (0, LANES - v.shape[0])).reshape(1, LANES)


def kernel(x, mem, norm_mix, w_in, conv_a_w, conv_a_b, ln_a_g, ln_a_b, dn_conv_w, dn_a_log, dn_dt_bias, dn_norm_g, gm_ln_g, gm_ln_b, gm_ws, gm_bs, pool_w, pool_scale, w_branch, w_out, norm_xa, norm_mem, xa_wq, xa_wkv, xa_wo, norm_mlp, mlp_w1, mlp_w2, norm_f):
    B, S, D = x.shape
    depth = w_in.shape[0]
    assert D == D_MODEL and S % DN_CHUNK == 0 and S % GM_CHUNK == 0
    T = B * S
    xf = x.reshape(T, D).astype(F32)
    memf = mem.reshape(B * mem.shape[1], D).astype(F32)
    row = lambda v: v.astype(F32).reshape(1, -1)

    for l in range(depth):
        w_main, w_small = _permute_w_in(w_in[l])
        z, sm = _inproj(xf, row(norm_mix[l]), w_main, w_small)
        bs_full = jnp.repeat(gm_bs[l].astype(F32).T, MIX_W // GM_GROUPS, axis=1)
        acp = _local_mixers(z, B, S, conv_a_w[l].astype(F32), row(conv_a_b[l]), row(ln_a_g[l]),
                            row(ln_a_b[l]), row(gm_ln_g[l]), row(gm_ln_b[l]), gm_ws[l].astype(F32),
                            bs_full, pool_w[l].astype(BF16), row(pool_scale[l]))
        dn = _deltanet(z, sm, B, S, dn_conv_w[l].astype(F32), _pad_lanes(dn_a_log[l]),
                       _pad_lanes(dn_dt_bias[l]), row(dn_norm_g[l]))
        xf = _merge(xf, acp, dn, z, w_branch[l].astype(BF16), w_out[l].astype(BF16))

        kv = _rms_matmul(memf, row(norm_mem[l]), xa_wkv[l].astype(BF16), 1024)
        xf = _xattn(xf, row(norm_xa[l]), xa_wq[l].astype(BF16), kv, xa_wo[l].astype(BF16), S)

        xf = _mlp(xf, row(norm_mlp[l]), mlp_w1[l].astype(BF16), mlp_w2[l].astype(BF16),
                  row(norm_f), final_norm=(l == depth - 1))
    return xf.reshape(B, S, D).astype(x.dtype)
```

```python
import functools

import jax
import jax.numpy as jnp
import numpy as np
from jax import lax
from jax.experimental import pallas as pl
from jax.experimental.pallas import tpu as pltpu

F32 = jnp.float32
BF16 = jnp.bfloat16

D_MODEL = 1024
MIX_W = D_MODEL // 2
N_BRANCH = 4
CONV_K = 31
DN_HEADS = 4
DN_HEAD_DIM = MIX_W // DN_HEADS
DN_CONV_K = 4
DN_CHUNK = 64
GM_GROUPS = 4
GM_CHUNK = 128
POOL_WINDOWS = (2, 4, 8, 16)
XA_HEADS = 4
XA_HEAD_DIM = D_MODEL // XA_HEADS
FFN_W = 4 * D_MODEL

V7X_VMEM_BYTES = 64 * 1024 * 1024
LANES = 128
SUBLANES = 8

Z_W = 2 * MIX_W + 2 * MIX_W + 4 * MIX_W + N_BRANCH * D_MODEL + MIX_W
Z_BN = 512
SMALL_W = 2 * LANES

CONV_HALO = 32
POOL_HALO = 32
DN_HALO = 8
LOCAL_TS = 512
DN_TS = 256


def _vmem_limit(block_bytes, scratch_bytes=0, temp_bytes=0):
    need = 2 * block_bytes + scratch_bytes + temp_bytes + (4 << 20)
    return int(min(need, V7X_VMEM_BYTES - (6 << 20)))


def _nbytes(shape, dtype):
    return int(np.prod(shape)) * jnp.dtype(dtype).itemsize


def _rmsnorm(x, g, eps=1e-6):
    return x * lax.rsqrt(jnp.mean(x * x, axis=-1, keepdims=True) + eps) * g


def _layernorm(x, g, b, eps=1e-5):
    mu = jnp.mean(x, axis=-1, keepdims=True)
    xc = x - mu
    var = jnp.mean(xc * xc, axis=-1, keepdims=True)
    return xc * lax.rsqrt(var + eps) * g + b


def _sigmoid(x):
    return 1.0 / (1.0 + jnp.exp(-x))


def _dot(a, b):
    return jnp.dot(a, b, preferred_element_type=F32)


def _dot_nt(a, b):
    return lax.dot_general(a, b, (((1,), (1,)), ((), ())), preferred_element_type=F32)


def _dot_tn(a, b):
    return lax.dot_general(a, b, (((0,), (0,)), ((), ())), preferred_element_type=F32)


def _inproj_kernel(x_ref, g_ref, w_ref, ws_ref, z_ref, sm_ref, h_ref):
    @pl.when(pl.program_id(1) == 0)
    def _():
        h = _rmsnorm(x_ref[...], g_ref[...]).astype(BF16)
        h_ref[...] = h
        sm_ref[...] = _dot(h, ws_ref[...])

    z_ref[...] = _dot(h_ref[...], w_ref[...]).astype(z_ref.dtype)


def _inproj(x, g, w, ws):
    T, D = x.shape
    bm = min(1024, T)
    nj = Z_W // Z_BN
    blocks = (_nbytes((bm, D), F32) + _nbytes((D, Z_BN), BF16) + _nbytes((D, SMALL_W), BF16)
              + _nbytes((bm, Z_BN), BF16) + _nbytes((bm, SMALL_W), F32))
    return pl.pallas_call(
        _inproj_kernel,
        grid=(T // bm, nj),
        in_specs=[
            pl.BlockSpec((bm, D), lambda i, j: (i, 0)),
            pl.BlockSpec((1, D), lambda i, j: (0, 0)),
            pl.BlockSpec((D, Z_BN), lambda i, j: (0, j)),
            pl.BlockSpec((D, SMALL_W), lambda i, j: (0, 0)),
        ],
        out_specs=[
            pl.BlockSpec((bm, Z_BN), lambda i, j: (i, j)),
            pl.BlockSpec((bm, SMALL_W), lambda i, j: (i, 0)),
        ],
        out_shape=[jax.ShapeDtypeStruct((T, Z_W), BF16), jax.ShapeDtypeStruct((T, SMALL_W), F32)],
        scratch_shapes=[pltpu.VMEM((bm, D), BF16)],
        compiler_params=pltpu.CompilerParams(
            dimension_semantics=("parallel", "arbitrary"),
            vmem_limit_bytes=_vmem_limit(blocks, _nbytes((bm, D), BF16), _nbytes((bm, D), F32))),
        name="inproj",
    )(x, g, w, ws)


def _rms_matmul_kernel(x_ref, g_ref, w_ref, o_ref, h_ref):
    @pl.when(pl.program_id(1) == 0)
    def _():
        h_ref[...] = _rmsnorm(x_ref[...], g_ref[...]).astype(BF16)

    o_ref[...] = _dot(h_ref[...], w_ref[...]).astype(o_ref.dtype)


def _rms_matmul(x, g, w, bn):
    T, D = x.shape
    N = w.shape[1]
    bm = min(1024, T)
    blocks = _nbytes((bm, D), F32) + _nbytes((D, bn), BF16) + _nbytes((bm, bn), BF16)
    return pl.pallas_call(
        _rms_matmul_kernel,
        grid=(T // bm, N // bn),
        in_specs=[
            pl.BlockSpec((bm, D), lambda i, j: (i, 0)),
            pl.BlockSpec((1, D), lambda i, j: (0, 0)),
            pl.BlockSpec((D, bn), lambda i, j: (0, j)),
        ],
        out_specs=pl.BlockSpec((bm, bn), lambda i, j: (i, j)),
        out_shape=jax.ShapeDtypeStruct((T, N), BF16),
        scratch_shapes=[pltpu.VMEM((bm, D), BF16)],
        compiler_params=pltpu.CompilerParams(
            dimension_semantics=("parallel", "arbitrary"),
            vmem_limit_bytes=_vmem_limit(blocks, _nbytes((bm, D), BF16), _nbytes((bm, D), F32))),
        name="mem_kv",
    )(x, g, w)


def _local_kernel(ain_ref, gm_ref, pin_ref, cw_ref, cb_ref, lag_ref, lab_ref, glg_ref, glb_ref,
                  ws_ref, bs_ref, pw_ref, ps_ref, out_ref, abuf, ybuf, pbuf, tbuf):
    s = pl.program_id(1)
    ts = ain_ref.shape[0]
    w = MIX_W

    @pl.when(s == 0)
    def _():
        abuf[0:CONV_HALO, :] = jnp.zeros((CONV_HALO, w), F32)
        pbuf[0:POOL_HALO, :] = jnp.zeros((POOL_HALO, w), F32)

    ain = ain_ref[...].astype(F32)
    abuf[CONV_HALO:CONV_HALO + ts, :] = ain[:, :w] * _sigmoid(ain[:, w:])
    rb = 32
    for r0 in range(0, ts, rb):
        acc = jnp.broadcast_to(cb_ref[...], (rb, w))
        for k in range(CONV_K):
            off = CONV_HALO - (CONV_K - 1) + k + r0
            acc = acc + cw_ref[k:k + 1, :] * abuf[off:off + rb, :]
        ybuf[r0:r0 + rb, :] = acc
    abuf[0:CONV_HALO, :] = abuf[ts:ts + CONV_HALO, :]
    ya = _layernorm(ybuf[...], lag_ref[...], lab_ref[...])
    out_ref[:, 0:w] = (ya * _sigmoid(ya)).astype(out_ref.dtype)

    gm = gm_ref[...].astype(F32)
    ge = 0.5 * gm * (1.0 + lax.erf(gm * np.float32(np.sqrt(0.5))))
    ybuf[...] = _layernorm(ge[:, w:], glg_ref[...], glb_ref[...])
    gw = w // GM_GROUPS
    row = lax.broadcasted_iota(jnp.int32, (GM_CHUNK, GM_CHUNK), 0)
    col = lax.broadcasted_iota(jnp.int32, (GM_CHUNK, GM_CHUNK), 1)
    for g in range(GM_GROUPS):
        wsg = jnp.where(row >= col, ws_ref[g], 0.0).astype(BF16)
        for n in range(ts // GM_CHUNK):
            rows = slice(n * GM_CHUNK, (n + 1) * GM_CHUNK)
            cols = slice(g * gw, (g + 1) * gw)
            mixed = _dot(wsg, ybuf[rows, cols].astype(BF16)) + bs_ref[:, cols]
            out_ref[rows, w + g * gw:w + (g + 1) * gw] = (ge[rows, cols] * mixed).astype(out_ref.dtype)

    pin = pin_ref[...].astype(F32)
    pbuf[POOL_HALO:POOL_HALO + ts, :] = pin
    tpos = s * ts + lax.broadcasted_iota(jnp.int32, (ts, 1), 0)
    n_ext = POOL_HALO + ts
    pw_cols = w // len(POOL_WINDOWS)
    for gi, win in enumerate(POOL_WINDOWS):
        cols = slice(gi * pw_cols, (gi + 1) * pw_cols)
        src = pbuf
        d = 1
        lo = SUBLANES
        while d < win:
            tbuf[lo:n_ext, cols] = src[lo:n_ext, cols] + src[lo - d:n_ext - d, cols]
            src = tbuf
            d *= 2
            lo += SUBLANES
        wsum = src[POOL_HALO:n_ext, cols]
        inv = 1.0 / jnp.minimum(tpos + 1, win).astype(F32)
        pg = (wsum * inv - pin[:, cols]).astype(BF16)
        lin = _dot(pg, pw_ref[gi]) * ps_ref[:, cols]
        out_ref[:, 2 * w + gi * pw_cols:2 * w + (gi + 1) * pw_cols] = lin.astype(out_ref.dtype)
    pbuf[0:POOL_HALO, :] = pbuf[ts:ts + POOL_HALO, :]


def _local_mixers(z, B, S, cw, cb, lag, lab, glg, glb, ws, bs_full, pw, ps):
    ts = min(LOCAL_TS, S)
    ns = S // ts
    w = MIX_W
    const = lambda *shape: pl.BlockSpec(shape, lambda b, s: (0,) * len(shape))
    blocks = (_nbytes((ts, 2 * w), BF16) * 2 + _nbytes((ts, w), BF16) + _nbytes((ts, 3 * w), BF16)
              + _nbytes((CONV_K + 8, w), F32) + _nbytes((GM_GROUPS, GM_CHUNK, GM_CHUNK), F32)
              + _nbytes((GM_CHUNK, w), F32) + _nbytes((4, 128, 128), BF16))
    scratch = (_nbytes((CONV_HALO + ts, w), F32) + _nbytes((ts, w), F32)
               + 2 * _nbytes((POOL_HALO + ts, w), F32))
    return pl.pallas_call(
        _local_kernel,
        grid=(B, ns),
        in_specs=[
            pl.BlockSpec((ts, 2 * w), lambda b, s: (b * ns + s, 0)),
            pl.BlockSpec((ts, 2 * w), lambda b, s: (b * ns + s, 1)),
            pl.BlockSpec((ts, w), lambda b, s: (b * ns + s, (Z_W - w) // w)),
            const(CONV_K, w), const(1, w), const(1, w), const(1, w), const(1, w), const(1, w),
            const(GM_GROUPS, GM_CHUNK, GM_CHUNK), const(GM_CHUNK, w),
            const(len(POOL_WINDOWS), w // 4, w // 4), const(1, w),
        ],
        out_specs=pl.BlockSpec((ts, 3 * w), lambda b, s: (b * ns + s, 0)),
        out_shape=jax.ShapeDtypeStruct((B * S, 3 * w), BF16),
        scratch_shapes=[
            pltpu.VMEM((CONV_HALO + ts, w), F32),
            pltpu.VMEM((ts, w), F32),
            pltpu.VMEM((POOL_HALO + ts, w), F32),
            pltpu.VMEM((POOL_HALO + ts, w), F32),
        ],
        compiler_params=pltpu.CompilerParams(
            dimension_semantics=("parallel", "arbitrary"),
            vmem_limit_bytes=_vmem_limit(blocks, scratch, 8 * _nbytes((ts, 2 * w), F32))),
        name="local_mixers",
    )(z, z, z, cw, cb, lag, lab, glg, glb, ws, bs_full, pw, ps)


def _split3(x):
    hi = x.astype(BF16)
    r1 = x - hi.astype(F32)
    mid = r1.astype(BF16)
    lo = (r1 - mid.astype(F32)).astype(BF16)
    return hi, mid, lo


def _dot_mask(mask_bf16, x):
    hi, mid, lo = _split3(x)
    return _dot(mask_bf16, hi) + _dot(mask_bf16, mid) + _dot(mask_bf16, lo)


def _deltanet_kernel(qk_ref, vd_ref, sm_ref, cw_ref, alog_ref, dtb_ref, ng_ref, o_ref,
                     xbuf, state, sol_s, vnew_s, qs_s):
    s = pl.program_id(1)
    ts = qk_ref.shape[0]
    w = MIX_W
    dh = DN_HEAD_DIM
    ck = DN_CHUNK

    @pl.when(s == 0)
    def _():
        xbuf[0:DN_HALO, :] = jnp.zeros((DN_HALO, 3 * w), F32)
        state[...] = jnp.zeros(state.shape, F32)

    xbuf[DN_HALO:DN_HALO + ts, 0:2 * w] = qk_ref[...].astype(F32)
    xbuf[DN_HALO:DN_HALO + ts, 2 * w:3 * w] = vd_ref[:, 0:w].astype(F32)
    y = jnp.zeros((ts, 3 * w), F32)
    for k in range(DN_CONV_K):
        off = DN_HALO - (DN_CONV_K - 1) + k
        y = y + cw_ref[k:k + 1, :] * xbuf[off:off + ts, :]
    xbuf[0:DN_HALO, :] = xbuf[ts:ts + DN_HALO, :]
    qkv = y * _sigmoid(y)

    sm = sm_ref[...]
    beta = _sigmoid(sm[:, 0:LANES])
    da = sm[:, LANES:2 * LANES] + dtb_ref[...]
    softplus = jnp.maximum(da, 0.0) + jnp.log1p(jnp.exp(-jnp.abs(da)))
    g = -jnp.exp(alog_ref[...]) * softplus

    row = lax.broadcasted_iota(jnp.int32, (ts, ts), 0)
    col = lax.broadcasted_iota(jnp.int32, (ts, ts), 1)
    rc = row ^ col
    same_chunk = (rc >> (ck.bit_length() - 1)) == 0
    causal = same_chunk & (row >= col)
    gam = _dot_mask(causal.astype(BF16), g)
    glast = _dot_mask(same_chunk.astype(BF16), g)
    gam_t = gam.T
    heads = range(DN_HEADS)

    a, attn16, rhs16, qdec, kdec = [], [], [], [], []
    for h in heads:
        qh = qkv[:, h * dh:(h + 1) * dh]
        kh = qkv[:, w + h * dh:w + (h + 1) * dh]
        vh = qkv[:, 2 * w + h * dh:2 * w + (h + 1) * dh]
        qn = qh * lax.rsqrt(jnp.sum(qh * qh, axis=-1, keepdims=True) + 1e-6) * np.float32(dh ** -0.5)
        kn = kh * lax.rsqrt(jnp.sum(kh * kh, axis=-1, keepdims=True) + 1e-6)
        bcol = beta[:, h:h + 1]
        gcol = gam[:, h:h + 1]
        kb = kn * bcol
        kn16 = kn.astype(BF16)
        decay = jnp.where(causal, jnp.exp(jnp.where(causal, gcol - gam_t[h:h + 1, :], 0.0)), 0.0)
        a.append(_dot_nt(kb.astype(BF16), kn16) * decay)
        attn16.append((_dot_nt(qn.astype(BF16), kn16) * decay).astype(BF16))
        eg = jnp.exp(gcol)
        rhs16.append(jnp.concatenate([vh * bcol, kb * eg], axis=1).astype(BF16))
        qdec.append((qn * eg).astype(BF16))
        kdec.append((kn * jnp.exp(glast[:, h:h + 1] - gcol)).astype(BF16))

    def low_left(m):
        return ((rc >> m.bit_length()) == 0) & ((row & m) != 0) & ((col & m) == 0)

    eye = (row == col).astype(F32)
    x = [eye - jnp.where(low_left(1), a[h], 0.0) for h in heads]
    m = 2
    while m < ck:
        mask = low_left(m)
        x16 = [x[h].astype(BF16) for h in heads]
        y16 = [_dot(jnp.where(mask, a[h], 0.0).astype(BF16), x16[h]).astype(BF16) for h in heads]
        x = [x[h] - _dot(x16[h], y16[h]) for h in heads]
        m *= 2
    for h in heads:
        sol_s[h] = _dot(x[h].astype(BF16), rhs16[h])

    st = [state[h] for h in heads]
    for c in range(ts // ck):
        rows = slice(c * ck, (c + 1) * ck)
        for h in heads:
            wq = jnp.concatenate([sol_s[h, rows, dh:2 * dh].astype(BF16), qdec[h][rows, :]], axis=0)
            r = _dot(wq, st[h].astype(BF16))
            vnew = sol_s[h, rows, 0:dh] - r[0:ck, :]
            vnew_s[h, rows, :] = vnew.astype(BF16)
            qs_s[h, rows, :] = r[ck:2 * ck, :]
            scale = jnp.exp(glast[c * ck:c * ck + 1, h:h + 1])
            st[h] = st[h] * scale + _dot_tn(kdec[h][rows, :], vnew.astype(BF16))
    for h in heads:
        state[h] = st[h]
        o = qs_s[h] + _dot(attn16[h], vnew_s[h])
        on = _rmsnorm(o, ng_ref[...])
        dg = vd_ref[:, w + h * dh:w + (h + 1) * dh].astype(F32)
        o_ref[:, h * dh:(h + 1) * dh] = (on * (dg * _sigmoid(dg))).astype(o_ref.dtype)


def _deltanet(z, sm, B, S, cw, alog, dtb, ng):
    ts = min(DN_TS, S)
    ns = S // ts
    w = MIX_W
    const = lambda *shape: pl.BlockSpec(shape, lambda b, s: (0,) * len(shape))
    blocks = (2 * _nbytes((ts, 2 * w), BF16) + _nbytes((ts, SMALL_W), F32) + _nbytes((8, 3 * w), F32)
              + _nbytes((ts, w), BF16))
    scratch = (_nbytes((DN_HALO + ts, 3 * w), F32) + _nbytes((DN_HEADS, DN_HEAD_DIM, DN_HEAD_DIM), F32)
               + DN_HEADS * _nbytes((ts, 4 * DN_HEAD_DIM), F32))
    return pl.pallas_call(
        _deltanet_kernel,
        grid=(B, ns),
        in_specs=[
            pl.BlockSpec((ts, 2 * w), lambda b, s: (b * ns + s, 2)),
            pl.BlockSpec((ts, 2 * w), lambda b, s: (b * ns + s, 3)),
            pl.BlockSpec((ts, SMALL_W), lambda b, s: (b * ns + s, 0)),
            const(DN_CONV_K, 3 * w), const(1, LANES), const(1, LANES), const(1, DN_HEAD_DIM),
        ],
        out_specs=pl.BlockSpec((ts, w), lambda b, s: (b * ns + s, 0)),
        out_shape=jax.ShapeDtypeStruct((B * S, w), BF16),
        scratch_shapes=[
            pltpu.VMEM((DN_HALO + ts, 3 * w), F32),
            pltpu.VMEM((DN_HEADS, DN_HEAD_DIM, DN_HEAD_DIM), F32),
            pltpu.VMEM((DN_HEADS, ts, 2 * DN_HEAD_DIM), F32),
            pltpu.VMEM((DN_HEADS, ts, DN_HEAD_DIM), BF16),
            pltpu.VMEM((DN_HEADS, ts, DN_HEAD_DIM), F32),
        ],
        compiler_params=pltpu.CompilerParams(
            dimension_semantics=("parallel", "arbitrary"),
            vmem_limit_bytes=_vmem_limit(blocks, scratch, 24 * _nbytes((ts, ts), F32)
                                         + 4 * _nbytes((ts, 3 * w), F32))),
        name="deltanet",
    )(z, z, sm, cw, alog, dtb, ng)


def _merge_kernel(x_ref, acp_ref, dn_ref, g0_ref, g1_ref, g2_ref, g3_ref, wb_ref, wo_ref, out_ref):
    w = MIX_W
    branches = (acp_ref[:, 0:w], dn_ref[...], acp_ref[:, w:2 * w], acp_ref[:, 2 * w:3 * w])
    gates = (g0_ref, g1_ref, g2_ref, g3_ref)
    merged = None
    for n in range(N_BRANCH):
        term = _sigmoid(gates[n][...].astype(F32)) * _dot(branches[n], wb_ref[n])
        merged = term if merged is None else merged + term
    out_ref[...] = x_ref[...] + _dot(merged.astype(BF16), wo_ref[...])


def _merge(x, acp, dn, z, wb, wo):
    T, D = x.shape
    w = MIX_W
    bm = min(512, T)
    gate_blk0 = (4 * 2 * w) // D
    gate_spec = lambda n: pl.BlockSpec((bm, D), lambda i: (i, gate_blk0 + n))
    blocks = (2 * _nbytes((bm, D), F32) + _nbytes((bm, 4 * w), BF16) + 4 * _nbytes((bm, D), BF16)
              + _nbytes((N_BRANCH, w, D), BF16) + _nbytes((D, D), BF16))
    return pl.pallas_call(
        _merge_kernel,
        grid=(T // bm,),
        in_specs=[
            pl.BlockSpec((bm, D), lambda i: (i, 0)),
            pl.BlockSpec((bm, 3 * w), lambda i: (i, 0)),
            pl.BlockSpec((bm, w), lambda i: (i, 0)),
            gate_spec(0), gate_spec(1), gate_spec(2), gate_spec(3),
            pl.BlockSpec((N_BRANCH, w, D), lambda i: (0, 0, 0)),
            pl.BlockSpec((D, D), lambda i: (0, 0)),
        ],
        out_specs=pl.BlockSpec((bm, D), lambda i: (i, 0)),
        out_shape=jax.ShapeDtypeStruct((T, D), F32),
        compiler_params=pltpu.CompilerParams(
            dimension_semantics=("parallel",),
            vmem_limit_bytes=_vmem_limit(blocks, 0, 4 * _nbytes((bm, D), F32))),
        name="merge",
    )(x, acp, dn, z, z, z, z, wb, wo)


def _xattn_kernel(x_ref, g_ref, wq_ref, kv_ref, wo_ref, out_ref):
    x = x_ref[...]
    q = _dot(_rmsnorm(x, g_ref[...]).astype(BF16), wq_ref[...])
    hd = XA_HEAD_DIM
    outs = []
    for h in range(XA_HEADS):
        qh = q[:, h * hd:(h + 1) * hd].astype(BF16)
        kh = kv_ref[:, h * hd:(h + 1) * hd]
        vh = kv_ref[:, D_MODEL + h * hd:D_MODEL + (h + 1) * hd]
        sc = _dot_nt(qh, kh) * np.float32(hd ** -0.5)
        p = jnp.exp(sc - jnp.max(sc, axis=-1, keepdims=True))
        p = p / jnp.sum(p, axis=-1, keepdims=True)
        outs.append(_dot(p.astype(BF16), vh).astype(BF16))
    out_ref[...] = x + _dot(jnp.concatenate(outs, axis=1), wo_ref[...])


def _xattn(x, g, wq, kv, wo, S):
    T, D = x.shape
    M = kv.shape[0] // (T // S)
    bm = min(512, S)
    per_b = S // bm
    blocks = (2 * _nbytes((bm, D), F32) + 2 * _nbytes((D, D), BF16) + _nbytes((M, 2 * D), BF16))
    return pl.pallas_call(
        _xattn_kernel,
        grid=(T // bm,),
        in_specs=[
            pl.BlockSpec((bm, D), lambda i: (i, 0)),
            pl.BlockSpec((1, D), lambda i: (0, 0)),
            pl.BlockSpec((D, D), lambda i: (0, 0)),
            pl.BlockSpec((M, 2 * D), lambda i: (i // per_b, 0)),
            pl.BlockSpec((D, D), lambda i: (0, 0)),
        ],
        out_specs=pl.BlockSpec((bm, D), lambda i: (i, 0)),
        out_shape=jax.ShapeDtypeStruct((T, D), F32),
        compiler_params=pltpu.CompilerParams(
            dimension_semantics=("parallel",),
            vmem_limit_bytes=_vmem_limit(blocks, 0, 6 * _nbytes((bm, D), F32))),
        name="xattn",
    )(x, g, wq, kv, wo)


def _mlp_kernel(x_ref, g_ref, w1_ref, w2_ref, gf_ref, out_ref, h_ref, *, final_norm):
    f = pl.program_id(1)

    @pl.when(f == 0)
    def _():
        x = x_ref[...]
        h_ref[...] = _rmsnorm(x, g_ref[...]).astype(BF16)
        out_ref[...] = x

    t = jnp.maximum(_dot(h_ref[...], w1_ref[...]), 0.0)
    out_ref[...] += _dot((t * t).astype(BF16), w2_ref[...])

    if final_norm:
        @pl.when(f == pl.num_programs(1) - 1)
        def _():
            out_ref[...] = _rmsnorm(out_ref[...], gf_ref[...])


def _mlp(x, g, w1, w2, gf, final_norm):
    T, D = x.shape
    F = w1.shape[1]
    bm = min(1024, T)
    bf = 512
    blocks = (2 * _nbytes((bm, D), F32) + 2 * _nbytes((D, bf), BF16))
    return pl.pallas_call(
        functools.partial(_mlp_kernel, final_norm=final_norm),
        grid=(T // bm, F // bf),
        in_specs=[
            pl.BlockSpec((bm, D), lambda i, f: (i, 0)),
            pl.BlockSpec((1, D), lambda i, f: (0, 0)),
            pl.BlockSpec((D, bf), lambda i, f: (0, f)),
            pl.BlockSpec((bf, D), lambda i, f: (f, 0)),
            pl.BlockSpec((1, D), lambda i, f: (0, 0)),
        ],
        out_specs=pl.BlockSpec((bm, D), lambda i, f: (i, 0)),
        out_shape=jax.ShapeDtypeStruct((T, D), F32),
        scratch_shapes=[pltpu.VMEM((bm, D), BF16)],
        compiler_params=pltpu.CompilerParams(
            dimension_semantics=("parallel", "arbitrary"),
            vmem_limit_bytes=_vmem_limit(blocks, _nbytes((bm, D), BF16), 3 * _nbytes((bm, bf), F32))),
        name="mlp",
    )(x, g, w1, w2, gf)


def _permute_w_in(w_in):
    w = MIX_W
    sizes = [2 * w, w, w, w, w, DN_HEADS, DN_HEADS, 2 * w, w, N_BRANCH * D_MODEL]
    idx = np.cumsum([0] + sizes)
    seg = lambda i: w_in[:, idx[i]:idx[i + 1]]
    a_in, dq, dk, dv, dgate, dbeta, da, gm_in, pool_in, gate_in = (seg(i) for i in range(10))
    main = jnp.concatenate([a_in, gm_in, dq, dk, dv, dgate, gate_in, pool_in], axis=1).astype(BF16)
    pad = jnp.zeros((w_in.shape[0], LANES - DN_HEADS), w_in.dtype)
    small = jnp.concatenate([dbeta, pad, da, pad], axis=1).astype(BF16)
    return main, small


def _pad_lanes(v):
    return jnp.pad(v.astype(F32), (0, LANES - v.shape[0])).reshape(1, LANES)


def kernel(x, mem, norm_mix, w_in, conv_a_w, conv_a_b, ln_a_g, ln_a_b, dn_conv_w, dn_a_log, dn_dt_bias, dn_norm_g, gm_ln_g, gm_ln_b, gm_ws, gm_bs, pool_w, pool_scale, w_branch, w_out, norm_xa, norm_mem, xa_wq, xa_wkv, xa_wo, norm_mlp, mlp_w1, mlp_w2, norm_f):
    B, S, D = x.shape
    depth = w_in.shape[0]
    assert D == D_MODEL and S % DN_CHUNK == 0 and S % GM_CHUNK == 0
    T = B * S
    xf = x.reshape(T, D).astype(F32)
    memf = mem.reshape(B * mem.shape[1], D).astype(F32)
    row = lambda v: v.astype(F32).reshape(1, -1)

    for l in range(depth):
        w_main, w_small = _permute_w_in(w_in[l])
        z, sm = _inproj(xf, row(norm_mix[l]), w_main, w_small)
        bs_full = jnp.repeat(gm_bs[l].astype(F32).T, MIX_W // GM_GROUPS, axis=1)
        acp = _local_mixers(z, B, S, conv_a_w[l].astype(F32), row(conv_a_b[l]), row(ln_a_g[l]),
                            row(ln_a_b[l]), row(gm_ln_g[l]), row(gm_ln_b[l]), gm_ws[l].astype(F32),
                            bs_full, pool_w[l].astype(BF16), row(pool_scale[l]))
        dn = _deltanet(z, sm, B, S, dn_conv_w[l].astype(F32), _pad_lanes(dn_a_log[l]),
                       _pad_lanes(dn_dt_bias[l]), row(dn_norm_g[l]))
        xf = _merge(xf, acp, dn, z, w_branch[l].astype(BF16), w_out[l].astype(BF16))

        kv = _rms_matmul(memf, row(norm_mem[l]), xa_wkv[l].astype(BF16), 1024)
        xf = _xattn(xf, row(norm_xa[l]), xa_wq[l].astype(BF16), kv, xa_wo[l].astype(BF16), S)

        xf = _mlp(xf, row(norm_mlp[l]), mlp_w1[l].astype(BF16), mlp_w2[l].astype(BF16),
                  row(norm_f), final_norm=(l == depth - 1))
    return xf.reshape(B, S, D).astype(x.dtype)
```

```python
import functools

import jax
import jax.numpy as jnp
import numpy as np
from jax import lax
from jax.experimental import pallas as pl
from jax.experimental.pallas import tpu as pltpu

F32 = jnp.float32
BF16 = jnp.bfloat16

D_MODEL = 1024
MIX_W = D_MODEL // 2
N_BRANCH = 4
CONV_K = 31
DN_HEADS = 4
DN_HEAD_DIM = MIX_W // DN_HEADS
DN_CONV_K = 4
DN_CHUNK = 64
GM_GROUPS = 4
GM_CHUNK = 128
POOL_WINDOWS = (2, 4, 8, 16)
XA_HEADS = 4
XA_HEAD_DIM = D_MODEL // XA_HEADS
FFN_W = 4 * D_MODEL

V7X_VMEM_BYTES = 64 * 1024 * 1024
LANES = 128
SUBLANES = 8

Z_W = 2 * MIX_W + 2 * MIX_W + 4 * MIX_W + N_BRANCH * D_MODEL + MIX_W
Z_BN = Z_W // 4
SMALL_W = 2 * LANES

CONV_HALO = 32
POOL_HALO = 32
DN_HALO = 8
LOCAL_TS = 512
DN_TS = 256


def _vmem_limit(block_bytes, scratch_bytes=0, temp_bytes=0):
    need = 2 * block_bytes + scratch_bytes + temp_bytes + (4 << 20)
    return int(min(need, V7X_VMEM_BYTES - (6 << 20)))


def _nbytes(shape, dtype):
    return int(np.prod(shape)) * jnp.dtype(dtype).itemsize


def _rmsnorm(x, g, eps=1e-6):
    return x * lax.rsqrt(jnp.mean(x * x, axis=-1, keepdims=True) + eps) * g


def _layernorm(x, g, b, eps=1e-5):
    mu = jnp.mean(x, axis=-1, keepdims=True)
    xc = x - mu
    var = jnp.mean(xc * xc, axis=-1, keepdims=True)
    return xc * lax.rsqrt(var + eps) * g + b


def _sigmoid(x):
    return 1.0 / (1.0 + jnp.exp(-x))


def _dot(a, b):
    return jnp.dot(a, b, preferred_element_type=F32)


def _dot_nt(a, b):
    return lax.dot_general(a, b, (((1,), (1,)), ((), ())), preferred_element_type=F32)


def _dot_tn(a, b):
    return lax.dot_general(a, b, (((0,), (0,)), ((), ())), preferred_element_type=F32)


def _inproj_kernel(x_ref, g_ref, w_ref, ws_ref, z_ref, sm_ref, h_ref):
    @pl.when(pl.program_id(1) == 0)
    def _():
        h = _rmsnorm(x_ref[...], g_ref[...]).astype(BF16)
        h_ref[...] = h
        sm_ref[...] = _dot(h, ws_ref[...])

    z_ref[...] = _dot(h_ref[...], w_ref[...]).astype(z_ref.dtype)


def _inproj(x, g, w, ws):
    T, D = x.shape
    bm = min(1024, T)
    nj = Z_W // Z_BN
    blocks = (_nbytes((bm, D), F32) + _nbytes((D, Z_BN), BF16) + _nbytes((D, SMALL_W), BF16)
              + _nbytes((bm, Z_BN), BF16) + _nbytes((bm, SMALL_W), F32))
    return pl.pallas_call(
        _inproj_kernel,
        grid=(T // bm, nj),
        in_specs=[
            pl.BlockSpec((bm, D), lambda i, j: (i, 0)),
            pl.BlockSpec((1, D), lambda i, j: (0, 0)),
            pl.BlockSpec((D, Z_BN), lambda i, j: (0, j)),
            pl.BlockSpec((D, SMALL_W), lambda i, j: (0, 0)),
        ],
        out_specs=[
            pl.BlockSpec((bm, Z_BN), lambda i, j: (i, j)),
            pl.BlockSpec((bm, SMALL_W), lambda i, j: (i, 0)),
        ],
        out_shape=[jax.ShapeDtypeStruct((T, Z_W), BF16), jax.ShapeDtypeStruct((T, SMALL_W), F32)],
        scratch_shapes=[pltpu.VMEM((bm, D), BF16)],
        compiler_params=pltpu.CompilerParams(
            dimension_semantics=("parallel", "arbitrary"),
            vmem_limit_bytes=_vmem_limit(blocks, _nbytes((bm, D), BF16),
                                         _nbytes((bm, D), F32) + _nbytes((bm, Z_BN), F32))),
        name="inproj",
    )(x, g, w, ws)


def _rms_matmul_kernel(x_ref, g_ref, w_ref, o_ref, h_ref):
    @pl.when(pl.program_id(1) == 0)
    def _():
        h_ref[...] = _rmsnorm(x_ref[...], g_ref[...]).astype(BF16)

    o_ref[...] = _dot(h_ref[...], w_ref[...]).astype(o_ref.dtype)


def _rms_matmul(x, g, w, bn):
    T, D = x.shape
    N = w.shape[1]
    bm = min(1024, T)
    blocks = _nbytes((bm, D), F32) + _nbytes((D, bn), BF16) + _nbytes((bm, bn), BF16)
    return pl.pallas_call(
        _rms_matmul_kernel,
        grid=(T // bm, N // bn),
        in_specs=[
            pl.BlockSpec((bm, D), lambda i, j: (i, 0)),
            pl.BlockSpec((1, D), lambda i, j: (0, 0)),
            pl.BlockSpec((D, bn), lambda i, j: (0, j)),
        ],
        out_specs=pl.BlockSpec((bm, bn), lambda i, j: (i, j)),
        out_shape=jax.ShapeDtypeStruct((T, N), BF16),
        scratch_shapes=[pltpu.VMEM((bm, D), BF16)],
        compiler_params=pltpu.CompilerParams(
            dimension_semantics=("parallel", "arbitrary"),
            vmem_limit_bytes=_vmem_limit(blocks, _nbytes((bm, D), BF16), _nbytes((bm, D), F32))),
        name="mem_kv",
    )(x, g, w)


def _local_kernel(ain_ref, gm_ref, pin_ref, cw_ref, cb_ref, lag_ref, lab_ref, glg_ref, glb_ref,
                  ws_ref, bs_ref, pw_ref, ps_ref, out_ref, abuf, ashift, cwb, ybuf, pbuf, tbuf):
    s = pl.program_id(1)
    ts = ain_ref.shape[0]
    w = MIX_W

    @pl.when(s == 0)
    def _():
        abuf[0:CONV_HALO, :] = jnp.zeros((CONV_HALO, w), F32)
        pbuf[0:POOL_HALO, :] = jnp.zeros((POOL_HALO, w), F32)
        for k in range(CONV_K):
            cwb[k] = jnp.broadcast_to(cw_ref[k:k + 1, :], (SUBLANES, w))
        cwb[CONV_K] = jnp.broadcast_to(cb_ref[...], (SUBLANES, w))

    ain = ain_ref[...].astype(F32)
    abuf[CONV_HALO:CONV_HALO + ts, :] = ain[:, :w] * _sigmoid(ain[:, w:])
    n_sh = CONV_HALO + ts - SUBLANES
    for r in range(1, SUBLANES):
        ashift[r - 1] = abuf[r:r + n_sh, :]
    rb = 4 * SUBLANES
    for r0 in range(0, ts, rb):
        acc = [cwb[CONV_K]] * (rb // SUBLANES)
        for k in range(CONV_K):
            off = CONV_HALO - (CONV_K - 1) + k + r0
            r = off % SUBLANES
            wk = cwb[k]
            for q in range(rb // SUBLANES):
                lo = off - r + q * SUBLANES
                src = abuf[lo:lo + SUBLANES, :] if r == 0 else ashift[r - 1, lo:lo + SUBLANES, :]
                acc[q] = acc[q] + wk * src
        for q in range(rb // SUBLANES):
            ybuf[r0 + q * SUBLANES:r0 + (q + 1) * SUBLANES, :] = acc[q]
    abuf[0:CONV_HALO, :] = abuf[ts:ts + CONV_HALO, :]
    ya = _layernorm(ybuf[...], lag_ref[...], lab_ref[...])
    out_ref[:, 0:w] = (ya * _sigmoid(ya)).astype(out_ref.dtype)

    gm = gm_ref[...].astype(F32)
    ge = 0.5 * gm * (1.0 + lax.erf(gm * np.float32(np.sqrt(0.5))))
    ybuf[...] = _layernorm(ge[:, w:], glg_ref[...], glb_ref[...])
    gw = w // GM_GROUPS
    row = lax.broadcasted_iota(jnp.int32, (GM_CHUNK, GM_CHUNK), 0)
    col = lax.broadcasted_iota(jnp.int32, (GM_CHUNK, GM_CHUNK), 1)
    for g in range(GM_GROUPS):
        wsg = jnp.where(row >= col, ws_ref[g], 0.0).astype(BF16)
        for n in range(ts // GM_CHUNK):
            rows = slice(n * GM_CHUNK, (n + 1) * GM_CHUNK)
            cols = slice(g * gw, (g + 1) * gw)
            mixed = _dot(wsg, ybuf[rows, cols].astype(BF16)) + bs_ref[:, cols]
            out_ref[rows, w + g * gw:w + (g + 1) * gw] = (ge[rows, cols] * mixed).astype(out_ref.dtype)

    pin = pin_ref[...].astype(F32)
    pbuf[POOL_HALO:POOL_HALO + ts, :] = pin
    tpos = s * ts + lax.broadcasted_iota(jnp.int32, (ts, 1), 0)
    n_ext = POOL_HALO + ts
    pw_cols = w // len(POOL_WINDOWS)
    for gi, win in enumerate(POOL_WINDOWS):
        cols = slice(gi * pw_cols, (gi + 1) * pw_cols)
        src = pbuf
        d = 1
        lo = SUBLANES
        while d < win:
            tbuf[lo:n_ext, cols] = src[lo:n_ext, cols] + src[lo - d:n_ext - d, cols]
            src = tbuf
            d *= 2
            lo += SUBLANES
        wsum = src[POOL_HALO:n_ext, cols]
        inv = 1.0 / jnp.minimum(tpos + 1, win).astype(F32)
        pg = (wsum * inv - pin[:, cols]).astype(BF16)
        lin = _dot(pg, pw_ref[gi]) * ps_ref[:, cols]
        out_ref[:, 2 * w + gi * pw_cols:2 * w + (gi + 1) * pw_cols] = lin.astype(out_ref.dtype)
    pbuf[0:POOL_HALO, :] = pbuf[ts:ts + POOL_HALO, :]


def _local_mixers(z, B, S, cw, cb, lag, lab, glg, glb, ws, bs_full, pw, ps):
    ts = min(LOCAL_TS, S)
    ns = S // ts
    w = MIX_W
    const = lambda *shape: pl.BlockSpec(shape, lambda b, s: (0,) * len(shape))
    blocks = (_nbytes((ts, 2 * w), BF16) * 2 + _nbytes((ts, w), BF16) + _nbytes((ts, 3 * w), BF16)
              + _nbytes((CONV_K + 8, w), F32) + _nbytes((GM_GROUPS, GM_CHUNK, GM_CHUNK), F32)
              + _nbytes((GM_CHUNK, w), F32) + _nbytes((4, 128, 128), BF16))
    scratch = (SUBLANES * _nbytes((CONV_HALO + ts, w), F32) + _nbytes((ts, w), F32)
               + 2 * _nbytes((POOL_HALO + ts, w), F32))
    return pl.pallas_call(
        _local_kernel,
        grid=(B, ns),
        in_specs=[
            pl.BlockSpec((ts, 2 * w), lambda b, s: (b * ns + s, 0)),
            pl.BlockSpec((ts, 2 * w), lambda b, s: (b * ns + s, 1)),
            pl.BlockSpec((ts, w), lambda b, s: (b * ns + s, (Z_W - w) // w)),
            const(CONV_K, w), const(1, w), const(1, w), const(1, w), const(1, w), const(1, w),
            const(GM_GROUPS, GM_CHUNK, GM_CHUNK), const(GM_CHUNK, w),
            const(len(POOL_WINDOWS), w // 4, w // 4), const(1, w),
        ],
        out_specs=pl.BlockSpec((ts, 3 * w), lambda b, s: (b * ns + s, 0)),
        out_shape=jax.ShapeDtypeStruct((B * S, 3 * w), BF16),
        scratch_shapes=[
            pltpu.VMEM((CONV_HALO + ts, w), F32),
            pltpu.VMEM((SUBLANES - 1, CONV_HALO + ts - SUBLANES, w), F32),
            pltpu.VMEM((CONV_K + 1, SUBLANES, w), F32),
            pltpu.VMEM((ts, w), F32),
            pltpu.VMEM((POOL_HALO + ts, w), F32),
            pltpu.VMEM((POOL_HALO + ts, w), F32),
        ],
        compiler_params=pltpu.CompilerParams(
            dimension_semantics=("parallel", "arbitrary"),
            vmem_limit_bytes=_vmem_limit(blocks, scratch, 8 * _nbytes((ts, 2 * w), F32))),
        name="local_mixers",
    )(z, z, z, cw, cb, lag, lab, glg, glb, ws, bs_full, pw, ps)


def _split3(x):
    hi = x.astype(BF16)
    r1 = x - hi.astype(F32)
    mid = r1.astype(BF16)
    lo = (r1 - mid.astype(F32)).astype(BF16)
    return hi, mid, lo


def _dot_mask(mask_bf16, x):
    hi, mid, lo = _split3(x)
    return _dot(mask_bf16, hi) + _dot(mask_bf16, mid) + _dot(mask_bf16, lo)


def _deltanet_masks(ts):
    row, col = np.indices((ts, ts))
    same = (row // DN_CHUNK) == (col // DN_CHUNK)
    causal = same & (row >= col)
    levels = []
    m = 1
    while m < DN_CHUNK:
        levels.append(((row // (2 * m)) == (col // (2 * m))) & ((row & m) != 0) & ((col & m) == 0))
        m *= 2
    mk = np.stack([causal, same]).astype(np.float32)
    mf = np.stack([causal] + levels + [row == col]).astype(np.float32)
    return jnp.asarray(mk, BF16), jnp.asarray(mf, F32)


def _deltanet_kernel(qk_ref, vd_ref, sm_ref, cw_ref, alog_ref, dtb_ref, ng_ref, mk_ref, mf_ref,
                     o_ref, xbuf, state, sol_s, vnew_s, qs_s):
    s = pl.program_id(1)
    ts = qk_ref.shape[0]
    w = MIX_W
    dh = DN_HEAD_DIM
    ck = DN_CHUNK

    @pl.when(s == 0)
    def _():
        xbuf[0:DN_HALO, :] = jnp.zeros((DN_HALO, 3 * w), F32)
        state[...] = jnp.zeros(state.shape, F32)

    xbuf[DN_HALO:DN_HALO + ts, 0:2 * w] = qk_ref[...].astype(F32)
    xbuf[DN_HALO:DN_HALO + ts, 2 * w:3 * w] = vd_ref[:, 0:w].astype(F32)
    y = jnp.zeros((ts, 3 * w), F32)
    for k in range(DN_CONV_K):
        off = DN_HALO - (DN_CONV_K - 1) + k
        y = y + cw_ref[k:k + 1, :] * xbuf[off:off + ts, :]
    xbuf[0:DN_HALO, :] = xbuf[ts:ts + DN_HALO, :]
    qkv = y * _sigmoid(y)

    sm = sm_ref[...]
    beta = _sigmoid(sm[:, 0:LANES])
    da = sm[:, LANES:2 * LANES] + dtb_ref[...]
    softplus = jnp.maximum(da, 0.0) + jnp.log1p(jnp.exp(-jnp.abs(da)))
    g = -jnp.exp(alog_ref[...]) * softplus

    gam = _dot_mask(mk_ref[0], g)
    glast = _dot_mask(mk_ref[1], g)
    gam_t = gam.T
    causal = mf_ref[0]
    heads = range(DN_HEADS)

    a, attn16, rhs16, qdec, kdec = [], [], [], [], []
    for h in heads:
        qh = qkv[:, h * dh:(h + 1) * dh]
        kh = qkv[:, w + h * dh:w + (h + 1) * dh]
        vh = qkv[:, 2 * w + h * dh:2 * w + (h + 1) * dh]
        qn = qh * lax.rsqrt(jnp.sum(qh * qh, axis=-1, keepdims=True) + 1e-6) * np.float32(dh ** -0.5)
        kn = kh * lax.rsqrt(jnp.sum(kh * kh, axis=-1, keepdims=True) + 1e-6)
        bcol = beta[:, h:h + 1]
        gcol = gam[:, h:h + 1]
        kb = kn * bcol
        kn16 = kn.astype(BF16)
        decay = jnp.exp((gcol - gam_t[h:h + 1, :]) * causal) * causal
        a.append(_dot_nt(kb.astype(BF16), kn16) * decay)
        attn16.append((_dot_nt(qn.astype(BF16), kn16) * decay).astype(BF16))
        eg = jnp.exp(gcol)
        rhs16.append(jnp.concatenate([vh * bcol, kb * eg], axis=1).astype(BF16))
        qdec.append((qn * eg).astype(BF16))
        kdec.append((kn * jnp.exp(glast[:, h:h + 1] - gcol)).astype(BF16))

    n_lvl = ck.bit_length() - 1
    x = [mf_ref[1 + n_lvl] - mf_ref[1] * a[h] for h in heads]
    for lvl in range(1, n_lvl):
        x16 = [x[h].astype(BF16) for h in heads]
        y16 = [_dot((mf_ref[1 + lvl] * a[h]).astype(BF16), x16[h]).astype(BF16) for h in heads]
        x = [x[h] - _dot(x16[h], y16[h]) for h in heads]
    for h in heads:
        sol_s[h] = _dot(x[h].astype(BF16), rhs16[h])

    st = [state[h] for h in heads]
    for c in range(ts // ck):
        rows = slice(c * ck, (c + 1) * ck)
        for h in heads:
            wq = jnp.concatenate([sol_s[h, rows, dh:2 * dh].astype(BF16), qdec[h][rows, :]], axis=0)
            r = _dot(wq, st[h].astype(BF16))
            vnew = sol_s[h, rows, 0:dh] - r[0:ck, :]
            vnew_s[h, rows, :] = vnew.astype(BF16)
            qs_s[h, rows, :] = r[ck:2 * ck, :]
            scale = jnp.exp(glast[c * ck:c * ck + 1, h:h + 1])
            st[h] = st[h] * scale + _dot_tn(kdec[h][rows, :], vnew.astype(BF16))
    for h in heads:
        state[h] = st[h]
        o = qs_s[h] + _dot(attn16[h], vnew_s[h])
        on = _rmsnorm(o, ng_ref[...])
        dg = vd_ref[:, w + h * dh:w + (h + 1) * dh].astype(F32)
        o_ref[:, h * dh:(h + 1) * dh] = (on * (dg * _sigmoid(dg))).astype(o_ref.dtype)


def _deltanet(z, sm, B, S, cw, alog, dtb, ng):
    ts = min(DN_TS, S)
    ns = S // ts
    w = MIX_W
    const = lambda *shape: pl.BlockSpec(shape, lambda b, s: (0,) * len(shape))
    mk, mf = _deltanet_masks(ts)
    blocks = (2 * _nbytes((ts, 2 * w), BF16) + _nbytes((ts, SMALL_W), F32) + _nbytes((8, 3 * w), F32)
              + _nbytes((ts, w), BF16) + _nbytes(mk.shape, BF16) + _nbytes(mf.shape, F32))
    scratch = (_nbytes((DN_HALO + ts, 3 * w), F32) + _nbytes((DN_HEADS, DN_HEAD_DIM, DN_HEAD_DIM), F32)
               + DN_HEADS * _nbytes((ts, 4 * DN_HEAD_DIM), F32))
    return pl.pallas_call(
        _deltanet_kernel,
        grid=(B, ns),
        in_specs=[
            pl.BlockSpec((ts, 2 * w), lambda b, s: (b * ns + s, 2)),
            pl.BlockSpec((ts, 2 * w), lambda b, s: (b * ns + s, 3)),
            pl.BlockSpec((ts, SMALL_W), lambda b, s: (b * ns + s, 0)),
            const(DN_CONV_K, 3 * w), const(1, LANES), const(1, LANES), const(1, DN_HEAD_DIM),
            const(*mk.shape), const(*mf.shape),
        ],
        out_specs=pl.BlockSpec((ts, w), lambda b, s: (b * ns + s, 0)),
        out_shape=jax.ShapeDtypeStruct((B * S, w), BF16),
        scratch_shapes=[
            pltpu.VMEM((DN_HALO + ts, 3 * w), F32),
            pltpu.VMEM((DN_HEADS, DN_HEAD_DIM, DN_HEAD_DIM), F32),
            pltpu.VMEM((DN_HEADS, ts, 2 * DN_HEAD_DIM), F32),
            pltpu.VMEM((DN_HEADS, ts, DN_HEAD_DIM), BF16),
            pltpu.VMEM((DN_HEADS, ts, DN_HEAD_DIM), F32),
        ],
        compiler_params=pltpu.CompilerParams(
            dimension_semantics=("parallel", "arbitrary"),
            vmem_limit_bytes=_vmem_limit(blocks, scratch, 24 * _nbytes((ts, ts), F32)
                                         + 4 * _nbytes((ts, 3 * w), F32))),
        name="deltanet",
    )(z, z, sm, cw, alog, dtb, ng, mk, mf)


def _merge_kernel(x_ref, acp_ref, dn_ref, g0_ref, g1_ref, g2_ref, g3_ref, wb_ref, wo_ref, out_ref):
    w = MIX_W
    branches = (acp_ref[:, 0:w], dn_ref[...], acp_ref[:, w:2 * w], acp_ref[:, 2 * w:3 * w])
    gates = (g0_ref, g1_ref, g2_ref, g3_ref)
    merged = None
    for n in range(N_BRANCH):
        term = _sigmoid(gates[n][...].astype(F32)) * _dot(branches[n], wb_ref[n])
        merged = term if merged is None else merged + term
    out_ref[...] = x_ref[...] + _dot(merged.astype(BF16), wo_ref[...])


def _merge(x, acp, dn, z, wb, wo):
    T, D = x.shape
    w = MIX_W
    bm = min(512, T)
    gate_blk0 = (4 * 2 * w) // D
    gate_spec = lambda n: pl.BlockSpec((bm, D), lambda i: (i, gate_blk0 + n))
    blocks = (2 * _nbytes((bm, D), F32) + _nbytes((bm, 4 * w), BF16) + 4 * _nbytes((bm, D), BF16)
              + _nbytes((N_BRANCH, w, D), BF16) + _nbytes((D, D), BF16))
    return pl.pallas_call(
        _merge_kernel,
        grid=(T // bm,),
        in_specs=[
            pl.BlockSpec((bm, D), lambda i: (i, 0)),
            pl.BlockSpec((bm, 3 * w), lambda i: (i, 0)),
            pl.BlockSpec((bm, w), lambda i: (i, 0)),
            gate_spec(0), gate_spec(1), gate_spec(2), gate_spec(3),
            pl.BlockSpec((N_BRANCH, w, D), lambda i: (0, 0, 0)),
            pl.BlockSpec((D, D), lambda i: (0, 0)),
        ],
        out_specs=pl.BlockSpec((bm, D), lambda i: (i, 0)),
        out_shape=jax.ShapeDtypeStruct((T, D), F32),
        compiler_params=pltpu.CompilerParams(
            dimension_semantics=("parallel",),
            vmem_limit_bytes=_vmem_limit(blocks, 0, 4 * _nbytes((bm, D), F32))),
        name="merge",
    )(x, acp, dn, z, z, z, z, wb, wo)


def _xattn_kernel(x_ref, g_ref, wq_ref, kv_ref, wo_ref, out_ref):
    x = x_ref[...]
    q = _dot(_rmsnorm(x, g_ref[...]).astype(BF16), wq_ref[...])
    hd = XA_HEAD_DIM
    outs = []
    for h in range(XA_HEADS):
        qh = q[:, h * hd:(h + 1) * hd].astype(BF16)
        kh = kv_ref[:, h * hd:(h + 1) * hd]
        vh = kv_ref[:, D_MODEL + h * hd:D_MODEL + (h + 1) * hd]
        sc = _dot_nt(qh, kh) * np.float32(hd ** -0.5)
        p = jnp.exp(sc - jnp.max(sc, axis=-1, keepdims=True))
        p = p / jnp.sum(p, axis=-1, keepdims=True)
        outs.append(_dot(p.astype(BF16), vh).astype(BF16))
    out_ref[...] = x + _dot(jnp.concatenate(outs, axis=1), wo_ref[...])


def _xattn(x, g, wq, kv, wo, S):
    T, D = x.shape
    M = kv.shape[0] // (T // S)
    bm = min(512, S)
    per_b = S // bm
    blocks = (2 * _nbytes((bm, D), F32) + 2 * _nbytes((D, D), BF16) + _nbytes((M, 2 * D), BF16))
    return pl.pallas_call(
        _xattn_kernel,
        grid=(T // bm,),
        in_specs=[
            pl.BlockSpec((bm, D), lambda i: (i, 0)),
            pl.BlockSpec((1, D), lambda i: (0, 0)),
            pl.BlockSpec((D, D), lambda i: (0, 0)),
            pl.BlockSpec((M, 2 * D), lambda i: (i // per_b, 0)),
            pl.BlockSpec((D, D), lambda i: (0, 0)),
        ],
        out_specs=pl.BlockSpec((bm, D), lambda i: (i, 0)),
        out_shape=jax.ShapeDtypeStruct((T, D), F32),
        compiler_params=pltpu.CompilerParams(
            dimension_semantics=("parallel",),
            vmem_limit_bytes=_vmem_limit(blocks, 0, 6 * _nbytes((bm, D), F32))),
        name="xattn",
    )(x, g, wq, kv, wo)


def _mlp_kernel(x_ref, g_ref, w1_ref, w2_ref, gf_ref, out_ref, h_ref, *, final_norm):
    f = pl.program_id(1)

    @pl.when(f == 0)
    def _():
        x = x_ref[...]
        h_ref[...] = _rmsnorm(x, g_ref[...]).astype(BF16)
        out_ref[...] = x

    t = jnp.maximum(_dot(h_ref[...], w1_ref[...]), 0.0)
    out_ref[...] += _dot((t * t).astype(BF16), w2_ref[...])

    if final_norm:
        @pl.when(f == pl.num_programs(1) - 1)
        def _():
            out_ref[...] = _rmsnorm(out_ref[...], gf_ref[...])


def _mlp(x, g, w1, w2, gf, final_norm):
    T, D = x.shape
    F = w1.shape[1]
    bm = min(1024, T)
    bf = 1024
    blocks = (2 * _nbytes((bm, D), F32) + 2 * _nbytes((D, bf), BF16))
    return pl.pallas_call(
        functools.partial(_mlp_kernel, final_norm=final_norm),
        grid=(T // bm, F // bf),
        in_specs=[
            pl.BlockSpec((bm, D), lambda i, f: (i, 0)),
            pl.BlockSpec((1, D), lambda i, f: (0, 0)),
            pl.BlockSpec((D, bf), lambda i, f: (0, f)),
            pl.BlockSpec((bf, D), lambda i, f: (f, 0)),
            pl.BlockSpec((1, D), lambda i, f: (0, 0)),
        ],
        out_specs=pl.BlockSpec((bm, D), lambda i, f: (i, 0)),
        out_shape=jax.ShapeDtypeStruct((T, D), F32),
        scratch_shapes=[pltpu.VMEM((bm, D), BF16)],
        compiler_params=pltpu.CompilerParams(
            dimension_semantics=("parallel", "arbitrary"),
            vmem_limit_bytes=_vmem_limit(blocks, _nbytes((bm, D), BF16), 3 * _nbytes((bm, bf), F32))),
        name="mlp",
    )(x, g, w1, w2, gf)


def _permute_w_in(w_in):
    w = MIX_W
    sizes = [2 * w, w, w, w, w, DN_HEADS, DN_HEADS, 2 * w, w, N_BRANCH * D_MODEL]
    idx = np.cumsum([0] + sizes)
    seg = lambda i: w_in[:, idx[i]:idx[i + 1]]
    a_in, dq, dk, dv, dgate, dbeta, da, gm_in, pool_in, gate_in = (seg(i) for i in range(10))
    main = jnp.concatenate([a_in, gm_in, dq, dk, dv, dgate, gate_in, pool_in], axis=1).astype(BF16)
    pad = jnp.zeros((w_in.shape[0], LANES - DN_HEADS), w_in.dtype)
    small = jnp.concatenate([dbeta, pad, da, pad], axis=1).astype(BF16)
    return main, small


def _pad_lanes(v):
    return jnp.pad(v.astype(F32), (0, LANES - v.shape[0])).reshape(1, LANES)


def kernel(x, mem, norm_mix, w_in, conv_a_w, conv_a_b, ln_a_g, ln_a_b, dn_conv_w, dn_a_log, dn_dt_bias, dn_norm_g, gm_ln_g, gm_ln_b, gm_ws, gm_bs, pool_w, pool_scale, w_branch, w_out, norm_xa, norm_mem, xa_wq, xa_wkv, xa_wo, norm_mlp, mlp_w1, mlp_w2, norm_f):
    B, S, D = x.shape
    depth = w_in.shape[0]
    assert D == D_MODEL and S % DN_CHUNK == 0 and S % GM_CHUNK == 0
    T = B * S
    xf = x.reshape(T, D).astype(F32)
    memf = mem.reshape(B * mem.shape[1], D).astype(F32)
    row = lambda v: v.astype(F32).reshape(1, -1)

    for l in range(depth):
        w_main, w_small = _permute_w_in(w_in[l])
        z, sm = _inproj(xf, row(norm_mix[l]), w_main, w_small)
        bs_full = jnp.repeat(gm_bs[l].astype(F32).T, MIX_W // GM_GROUPS, axis=1)
        acp = _local_mixers(z, B, S, conv_a_w[l].astype(F32), row(conv_a_b[l]), row(ln_a_g[l]),
                            row(ln_a_b[l]), row(gm_ln_g[l]), row(gm_ln_b[l]), gm_ws[l].astype(F32),
                            bs_full, pool_w[l].astype(BF16), row(pool_scale[l]))
        dn = _deltanet(z, sm, B, S, dn_conv_w[l].astype(F32), _pad_lanes(dn_a_log[l]),
                       _pad_lanes(dn_dt_bias[l]), row(dn_norm_g[l]))
        xf = _merge(xf, acp, dn, z, w_branch[l].astype(BF16), w_out[l].astype(BF16))

        kv = _rms_matmul(memf, row(norm_mem[l]), xa_wkv[l].astype(BF16), 1024)
        xf = _xattn(xf, row(norm_xa[l]), xa_wq[l].astype(BF16), kv, xa_wo[l].astype(BF16), S)

        xf = _mlp(xf, row(norm_mlp[l]), mlp_w1[l].astype(BF16), mlp_w2[l].astype(BF16),
                  row(norm_f), final_norm=(l == depth - 1))
    return xf.reshape(B, S, D).astype(x.dtype)
```

```python
import functools

import jax
import jax.numpy as jnp
import numpy as np
from jax import lax
from jax.experimental import pallas as pl
from jax.experimental.pallas import tpu as pltpu

F32 = jnp.float32
BF16 = jnp.bfloat16

D_MODEL = 1024
MIX_W = D_MODEL // 2
N_BRANCH = 4
CONV_K = 31
DN_HEADS = 4
DN_HEAD_DIM = MIX_W // DN_HEADS
DN_CONV_K = 4
DN_CHUNK = 64
GM_GROUPS = 4
GM_CHUNK = 128
POOL_WINDOWS = (2, 4, 8, 16)
XA_HEADS = 4
XA_HEAD_DIM = D_MODEL // XA_HEADS
FFN_W = 4 * D_MODEL

V7X_VMEM_BYTES = 64 * 1024 * 1024
LANES = 128
SUBLANES = 8

Z_W = 2 * MIX_W + 2 * MIX_W + 4 * MIX_W + N_BRANCH * D_MODEL + MIX_W
Z_BN = Z_W // 4
SMALL_W = 2 * LANES

CONV_HALO = 32
POOL_HALO = 32
DN_HALO = 8
LOCAL_TS = 512
DN_TS = 256
DN_DIAG = 128


def _vmem_limit(block_bytes, scratch_bytes=0, temp_bytes=0):
    need = 2 * block_bytes + scratch_bytes + temp_bytes + (4 << 20)
    return int(min(need, V7X_VMEM_BYTES - (6 << 20)))


def _nbytes(shape, dtype):
    return int(np.prod(shape)) * jnp.dtype(dtype).itemsize


def _rmsnorm(x, g, eps=1e-6):
    return x * lax.rsqrt(jnp.mean(x * x, axis=-1, keepdims=True) + eps) * g


def _layernorm(x, g, b, eps=1e-5):
    mu = jnp.mean(x, axis=-1, keepdims=True)
    xc = x - mu
    var = jnp.mean(xc * xc, axis=-1, keepdims=True)
    return xc * lax.rsqrt(var + eps) * g + b


def _sigmoid(x):
    return 0.5 * jnp.tanh(0.5 * x) + 0.5


def _swish(x):
    h = 0.5 * x
    return h * jnp.tanh(h) + h


def _dot(a, b):
    return jnp.dot(a, b, preferred_element_type=F32)


def _dot_nt(a, b):
    return lax.dot_general(a, b, (((1,), (1,)), ((), ())), preferred_element_type=F32)


def _dot_tn(a, b):
    return lax.dot_general(a, b, (((0,), (0,)), ((), ())), preferred_element_type=F32)


def _inproj_kernel(x_ref, g_ref, w_ref, ws_ref, z_ref, sm_ref, h_ref):
    @pl.when(pl.program_id(1) == 0)
    def _():
        h = _rmsnorm(x_ref[...], g_ref[...]).astype(BF16)
        h_ref[...] = h
        sm_ref[...] = _dot(h, ws_ref[...])

    z_ref[...] = _dot(h_ref[...], w_ref[...]).astype(z_ref.dtype)


def _inproj(x, g, w, ws):
    T, D = x.shape
    bm = min(1024, T)
    nj = Z_W // Z_BN
    blocks = (_nbytes((bm, D), F32) + _nbytes((D, Z_BN), BF16) + _nbytes((D, SMALL_W), BF16)
              + _nbytes((bm, Z_BN), BF16) + _nbytes((bm, SMALL_W), F32))
    return pl.pallas_call(
        _inproj_kernel,
        grid=(T // bm, nj),
        in_specs=[
            pl.BlockSpec((bm, D), lambda i, j: (i, 0)),
            pl.BlockSpec((1, D), lambda i, j: (0, 0)),
            pl.BlockSpec((D, Z_BN), lambda i, j: (0, j)),
            pl.BlockSpec((D, SMALL_W), lambda i, j: (0, 0)),
        ],
        out_specs=[
            pl.BlockSpec((bm, Z_BN), lambda i, j: (i, j)),
            pl.BlockSpec((bm, SMALL_W), lambda i, j: (i, 0)),
        ],
        out_shape=[jax.ShapeDtypeStruct((T, Z_W), BF16), jax.ShapeDtypeStruct((T, SMALL_W), F32)],
        scratch_shapes=[pltpu.VMEM((bm, D), BF16)],
        compiler_params=pltpu.CompilerParams(
            dimension_semantics=("parallel", "arbitrary"),
            vmem_limit_bytes=_vmem_limit(blocks, _nbytes((bm, D), BF16),
                                         _nbytes((bm, D), F32) + _nbytes((bm, Z_BN), F32))),
        name="inproj",
    )(x, g, w, ws)


def _rms_matmul_kernel(x_ref, g_ref, w_ref, o_ref, h_ref):
    @pl.when(pl.program_id(1) == 0)
    def _():
        h_ref[...] = _rmsnorm(x_ref[...], g_ref[...]).astype(BF16)

    o_ref[...] = _dot(h_ref[...], w_ref[...]).astype(o_ref.dtype)


def _rms_matmul(x, g, w, bn):
    T, D = x.shape
    N = w.shape[1]
    bm = min(1024, T)
    blocks = _nbytes((bm, D), F32) + _nbytes((D, bn), BF16) + _nbytes((bm, bn), BF16)
    return pl.pallas_call(
        _rms_matmul_kernel,
        grid=(T // bm, N // bn),
        in_specs=[
            pl.BlockSpec((bm, D), lambda i, j: (i, 0)),
            pl.BlockSpec((1, D), lambda i, j: (0, 0)),
            pl.BlockSpec((D, bn), lambda i, j: (0, j)),
        ],
        out_specs=pl.BlockSpec((bm, bn), lambda i, j: (i, j)),
        out_shape=jax.ShapeDtypeStruct((T, N), BF16),
        scratch_shapes=[pltpu.VMEM((bm, D), BF16)],
        compiler_params=pltpu.CompilerParams(
            dimension_semantics=("parallel", "arbitrary"),
            vmem_limit_bytes=_vmem_limit(blocks, _nbytes((bm, D), BF16), _nbytes((bm, D), F32))),
        name="mem_kv",
    )(x, g, w)


def _local_kernel(ain_ref, gm_ref, pin_ref, cw_ref, cb_ref, lag_ref, lab_ref, glg_ref, glb_ref,
                  ws_ref, bs_ref, pw_ref, ps_ref, out_ref, abuf, ashift, cwb, ybuf, pbuf, tbuf):
    s = pl.program_id(1)
    ts = ain_ref.shape[0]
    w = MIX_W

    @pl.when(s == 0)
    def _():
        abuf[0:CONV_HALO, :] = jnp.zeros((CONV_HALO, w), F32)
        pbuf[0:POOL_HALO, :] = jnp.zeros((POOL_HALO, w), F32)
        for k in range(CONV_K):
            cwb[k] = jnp.broadcast_to(cw_ref[k:k + 1, :], (SUBLANES, w))
        cwb[CONV_K] = jnp.broadcast_to(cb_ref[...], (SUBLANES, w))

    ain = ain_ref[...].astype(F32)
    abuf[CONV_HALO:CONV_HALO + ts, :] = ain[:, :w] * _sigmoid(ain[:, w:])
    n_sh = CONV_HALO + ts - SUBLANES
    for r in range(1, SUBLANES):
        ashift[r - 1] = abuf[r:r + n_sh, :]
    rb = 4 * SUBLANES
    for r0 in range(0, ts, rb):
        acc = [cwb[CONV_K]] * (rb // SUBLANES)
        for k in range(CONV_K):
            off = CONV_HALO - (CONV_K - 1) + k + r0
            r = off % SUBLANES
            wk = cwb[k]
            for q in range(rb // SUBLANES):
                lo = off - r + q * SUBLANES
                src = abuf[lo:lo + SUBLANES, :] if r == 0 else ashift[r - 1, lo:lo + SUBLANES, :]
                acc[q] = acc[q] + wk * src
        for q in range(rb // SUBLANES):
            ybuf[r0 + q * SUBLANES:r0 + (q + 1) * SUBLANES, :] = acc[q]
    abuf[0:CONV_HALO, :] = abuf[ts:ts + CONV_HALO, :]
    ya = _layernorm(ybuf[...], lag_ref[...], lab_ref[...])
    out_ref[:, 0:w] = _swish(ya).astype(out_ref.dtype)

    gm = gm_ref[...].astype(F32)
    ge = 0.5 * gm * (1.0 + lax.erf(gm * np.float32(np.sqrt(0.5))))
    ybuf[...] = _layernorm(ge[:, w:], glg_ref[...], glb_ref[...])
    gw = w // GM_GROUPS
    row = lax.broadcasted_iota(jnp.int32, (GM_CHUNK, GM_CHUNK), 0)
    col = lax.broadcasted_iota(jnp.int32, (GM_CHUNK, GM_CHUNK), 1)
    for g in range(GM_GROUPS):
        wsg = jnp.where(row >= col, ws_ref[g], 0.0).astype(BF16)
        for n in range(ts // GM_CHUNK):
            rows = slice(n * GM_CHUNK, (n + 1) * GM_CHUNK)
            cols = slice(g * gw, (g + 1) * gw)
            mixed = _dot(wsg, ybuf[rows, cols].astype(BF16)) + bs_ref[:, cols]
            out_ref[rows, w + g * gw:w + (g + 1) * gw] = (ge[rows, cols] * mixed).astype(out_ref.dtype)

    pin = pin_ref[...].astype(F32)
    pbuf[POOL_HALO:POOL_HALO + ts, :] = pin
    tpos = s * ts + lax.broadcasted_iota(jnp.int32, (ts, 1), 0)
    n_ext = POOL_HALO + ts
    pw_cols = w // len(POOL_WINDOWS)
    for gi, win in enumerate(POOL_WINDOWS):
        cols = slice(gi * pw_cols, (gi + 1) * pw_cols)
        src = pbuf
        d = 1
        lo = SUBLANES
        while d < win:
            tbuf[lo:n_ext, cols] = src[lo:n_ext, cols] + src[lo - d:n_ext - d, cols]
            src = tbuf
            d *= 2
            lo += SUBLANES
        wsum = src[POOL_HALO:n_ext, cols]
        inv = 1.0 / jnp.minimum(tpos + 1, win).astype(F32)
        pg = (wsum * inv - pin[:, cols]).astype(BF16)
        lin = _dot(pg, pw_ref[gi]) * ps_ref[:, cols]
        out_ref[:, 2 * w + gi * pw_cols:2 * w + (gi + 1) * pw_cols] = lin.astype(out_ref.dtype)
    pbuf[0:POOL_HALO, :] = pbuf[ts:ts + POOL_HALO, :]


def _local_mixers(z, B, S, cw, cb, lag, lab, glg, glb, ws, bs_full, pw, ps):
    ts = min(LOCAL_TS, S)
    ns = S // ts
    w = MIX_W
    const = lambda *shape: pl.BlockSpec(shape, lambda b, s: (0,) * len(shape))
    blocks = (_nbytes((ts, 2 * w), BF16) * 2 + _nbytes((ts, w), BF16) + _nbytes((ts, 3 * w), BF16)
              + _nbytes((CONV_K + 8, w), F32) + _nbytes((GM_GROUPS, GM_CHUNK, GM_CHUNK), F32)
              + _nbytes((GM_CHUNK, w), F32) + _nbytes((4, 128, 128), BF16))
    scratch = (SUBLANES * _nbytes((CONV_HALO + ts, w), F32) + _nbytes((ts, w), F32)
               + 2 * _nbytes((POOL_HALO + ts, w), F32))
    return pl.pallas_call(
        _local_kernel,
        grid=(B, ns),
        in_specs=[
            pl.BlockSpec((ts, 2 * w), lambda b, s: (b * ns + s, 0)),
            pl.BlockSpec((ts, 2 * w), lambda b, s: (b * ns + s, 1)),
            pl.BlockSpec((ts, w), lambda b, s: (b * ns + s, (Z_W - w) // w)),
            const(CONV_K, w), const(1, w), const(1, w), const(1, w), const(1, w), const(1, w),
            const(GM_GROUPS, GM_CHUNK, GM_CHUNK), const(GM_CHUNK, w),
            const(len(POOL_WINDOWS), w // 4, w // 4), const(1, w),
        ],
        out_specs=pl.BlockSpec((ts, 3 * w), lambda b, s: (b * ns + s, 0)),
        out_shape=jax.ShapeDtypeStruct((B * S, 3 * w), BF16),
        scratch_shapes=[
            pltpu.VMEM((CONV_HALO + ts, w), F32),
            pltpu.VMEM((SUBLANES - 1, CONV_HALO + ts - SUBLANES, w), F32),
            pltpu.VMEM((CONV_K + 1, SUBLANES, w), F32),
            pltpu.VMEM((ts, w), F32),
            pltpu.VMEM((POOL_HALO + ts, w), F32),
            pltpu.VMEM((POOL_HALO + ts, w), F32),
        ],
        compiler_params=pltpu.CompilerParams(
            dimension_semantics=("parallel", "arbitrary"),
            vmem_limit_bytes=_vmem_limit(blocks, scratch, 8 * _nbytes((ts, 2 * w), F32))),
        name="local_mixers",
    )(z, z, z, cw, cb, lag, lab, glg, glb, ws, bs_full, pw, ps)


def _split3(x):
    hi = x.astype(BF16)
    r1 = x - hi.astype(F32)
    mid = r1.astype(BF16)
    lo = (r1 - mid.astype(F32)).astype(BF16)
    return hi, mid, lo


def _dot_mask(mask_bf16, x):
    hi, mid, lo = _split3(x)
    return _dot(mask_bf16, hi) + _dot(mask_bf16, mid) + _dot(mask_bf16, lo)


def _deltanet_masks(ts):
    def causal_same(n):
        row, col = np.indices((n, n))
        same = (row // DN_CHUNK) == (col // DN_CHUNK)
        return same & (row >= col), same, row, col

    causal, same, _, _ = causal_same(ts)
    mk = np.stack([causal, same]).astype(np.float32)
    causal, _, row, col = causal_same(DN_DIAG)
    levels = []
    m = 1
    while m < DN_CHUNK:
        levels.append(((row // (2 * m)) == (col // (2 * m))) & ((row & m) != 0) & ((col & m) == 0))
        m *= 2
    mf = np.stack([causal] + levels + [row == col]).astype(np.float32)
    return jnp.asarray(mk, BF16), jnp.asarray(mf, F32)


def _blockdiag_bf16(squares):
    n = len(squares)
    zero = jnp.zeros(squares[0].shape, BF16)
    rows = [jnp.concatenate([squares[i].astype(BF16) if j == i else zero for j in range(n)], axis=1)
            for i in range(n)]
    return jnp.concatenate(rows, axis=0)


def _deltanet_front(qk_ref, vd_ref, sm_ref, cw_ref, alog_ref, dtb_ref, mk_ref, mf_ref, xbuf, qkv_s,
                    stage):
    a_s, attn_s, rhs_s, qdec_s, kdec_s, egl_s, gate_s = stage
    ts = qk_ref.shape[0]
    w = MIX_W
    dh = DN_HEAD_DIM

    for cg in range(3 * w // LANES):
        cols = slice(cg * LANES, (cg + 1) * LANES)
        src = qk_ref[:, cols] if cg < 2 * w // LANES else vd_ref[:, cg * LANES - 2 * w:(cg + 1) * LANES - 2 * w]
        xbuf[DN_HALO:DN_HALO + ts, cols] = src.astype(F32)
        y = None
        for k in range(DN_CONV_K):
            off = DN_HALO - (DN_CONV_K - 1) + k
            tap = cw_ref[k:k + 1, cols] * xbuf[off:off + ts, cols]
            y = tap if y is None else y + tap
        xbuf[0:DN_HALO, cols] = xbuf[ts:ts + DN_HALO, cols]
        qkv_s[:, cols] = _swish(y)
        if cg % 2 == 1:
            yield

    sm = sm_ref[...]
    beta = _sigmoid(sm[:, 0:LANES])
    da = sm[:, LANES:2 * LANES] + dtb_ref[...]
    softplus = jnp.maximum(da, 0.0) + jnp.log1p(jnp.exp(-jnp.abs(da)))
    g = -jnp.exp(alog_ref[...]) * softplus

    gam = _dot_mask(mk_ref[0], g)
    glast = _dot_mask(mk_ref[1], g)
    gam_t = gam.T
    causal = mf_ref[0]
    egl_s[...] = jnp.exp(glast)
    yield
    for h in range(DN_HEADS):
        dg = vd_ref[:, w + h * dh:w + (h + 1) * dh].astype(F32)
        gate_s[:, h * dh:(h + 1) * dh] = _swish(dg)
        qh = qkv_s[:, h * dh:(h + 1) * dh]
        kh = qkv_s[:, w + h * dh:w + (h + 1) * dh]
        vh = qkv_s[:, 2 * w + h * dh:2 * w + (h + 1) * dh]
        qn = qh * lax.rsqrt(jnp.sum(qh * qh, axis=-1, keepdims=True) + 1e-6) * np.float32(dh ** -0.5)
        kn = kh * lax.rsqrt(jnp.sum(kh * kh, axis=-1, keepdims=True) + 1e-6)
        bcol = beta[:, h:h + 1]
        gcol = gam[:, h:h + 1]
        kb = kn * bcol
        kn16 = kn.astype(BF16)
        kk = _dot_nt(kb.astype(BF16), kn16)
        qk = _dot_nt(qn.astype(BF16), kn16)
        for d in range(ts // DN_DIAG):
            dd = slice(d * DN_DIAG, (d + 1) * DN_DIAG)
            decay = jnp.exp((gcol[dd] - gam_t[h:h + 1, dd]) * causal) * causal
            a_s[h, d] = kk[dd, dd] * decay
            attn_s[h, dd, dd] = (qk[dd, dd] * decay).astype(BF16)
        eg = jnp.exp(gcol)
        rhs_s[h] = jnp.concatenate([vh * bcol, kb * eg], axis=1).astype(BF16)
        qdec_s[h] = (qn * eg).astype(BF16)
        kdec_s[h] = (kn * jnp.exp(glast[:, h:h + 1] - gcol)).astype(BF16)
        yield


def _deltanet_back(ng_ref, mf_ref, o_ref, stage, state, sol_s, vnew_s, qs_s):
    a_s, attn_s, rhs_s, qdec_s, kdec_s, egl_s, gate_s = stage
    ts = o_ref.shape[0]
    dh = DN_HEAD_DIM
    ck = DN_CHUNK
    heads = range(DN_HEADS)

    n_lvl = ck.bit_length() - 1
    diags = range(ts // DN_DIAG)
    sq = lambda m, d: m[d * DN_DIAG:(d + 1) * DN_DIAG, d * DN_DIAG:(d + 1) * DN_DIAG]
    x = [[mf_ref[1 + n_lvl] - mf_ref[1] * a_s[h, d] for d in diags] for h in heads]
    for lvl in range(1, n_lvl):
        x16 = [_blockdiag_bf16(x[h]) for h in heads]
        b16 = [_blockdiag_bf16([mf_ref[1 + lvl] * a_s[h, d] for d in diags]) for h in heads]
        y = [_dot(b16[h], x16[h]) for h in heads]
        y16 = [_blockdiag_bf16([sq(y[h], d) for d in diags]) for h in heads]
        xy = [_dot(x16[h], y16[h]) for h in heads]
        x = [[x[h][d] - sq(xy[h], d) for d in diags] for h in heads]
        yield
    for h in heads:
        sol_s[h] = _dot(_blockdiag_bf16(x[h]), rhs_s[h])
    yield

    st = [state[h] for h in heads]
    for c in range(ts // ck):
        rows = slice(c * ck, (c + 1) * ck)
        for h in heads:
            wq = jnp.concatenate([sol_s[h, rows, dh:2 * dh].astype(BF16), qdec_s[h, rows, :]], axis=0)
            r = _dot(wq, st[h].astype(BF16))
            vnew = sol_s[h, rows, 0:dh] - r[0:ck, :]
            vnew_s[h, rows, :] = vnew.astype(BF16)
            qs_s[h, rows, :] = r[ck:2 * ck, :]
            scale = egl_s[c * ck:c * ck + 1, h:h + 1]
            st[h] = st[h] * scale + _dot_tn(kdec_s[h, rows, :], vnew.astype(BF16))
        yield
    for h in heads:
        state[h] = st[h]
        o = qs_s[h] + _dot(attn_s[h], vnew_s[h])
        on = _rmsnorm(o, ng_ref[...])
        o_ref[:, h * dh:(h + 1) * dh] = (on * gate_s[:, h * dh:(h + 1) * dh]).astype(o_ref.dtype)


def _deltanet_kernel(qk_ref, vd_ref, sm_ref, cw_ref, alog_ref, dtb_ref, ng_ref, mk_ref, mf_ref,
                     o_ref, xbuf, qkv_s, state, sol_s, vnew_s, qs_s, *stages):
    s = pl.program_id(1)
    n_stage = len(stages) // 2
    stage_a, stage_b = stages[:n_stage], stages[n_stage:]

    @pl.when(s == 0)
    def _():
        xbuf[0:DN_HALO, :] = jnp.zeros((DN_HALO, xbuf.shape[1]), F32)
        state[...] = jnp.zeros(state.shape, F32)
        for ref in stage_a + stage_b:
            ref[...] = jnp.zeros(ref.shape, ref.dtype)

    def step(write_stage, read_stage):
        halves = [
            _deltanet_back(ng_ref, mf_ref, o_ref, read_stage, state, sol_s, vnew_s, qs_s),
            _deltanet_front(qk_ref, vd_ref, sm_ref, cw_ref, alog_ref, dtb_ref, mk_ref, mf_ref, xbuf,
                            qkv_s, write_stage),
        ]
        while halves:
            for half in list(halves):
                if next(half, StopIteration) is StopIteration:
                    halves.remove(half)

    @pl.when(s % 2 == 0)
    def _():
        step(stage_a, stage_b)

    @pl.when(s % 2 == 1)
    def _():
        step(stage_b, stage_a)


def _deltanet(z, sm, B, S, cw, alog, dtb, ng):
    ts = min(DN_TS, S)
    ns = S // ts
    w = MIX_W
    dh = DN_HEAD_DIM
    const = lambda *shape: pl.BlockSpec(shape, lambda b, s: (0,) * len(shape))
    mk, mf = _deltanet_masks(ts)
    in_row = lambda b, s: b * ns + jnp.minimum(s, ns - 1)
    out_row = lambda b, s: b * ns + jnp.maximum(s - 1, 0)
    stage = [
        pltpu.VMEM((DN_HEADS, ts // DN_DIAG, DN_DIAG, DN_DIAG), F32),
        pltpu.VMEM((DN_HEADS, ts, ts), BF16),
        pltpu.VMEM((DN_HEADS, ts, 2 * dh), BF16),
        pltpu.VMEM((DN_HEADS, ts, dh), BF16),
        pltpu.VMEM((DN_HEADS, ts, dh), BF16),
        pltpu.VMEM((ts, LANES), F32),
        pltpu.VMEM((ts, w), F32),
    ]
    stage_bytes = (DN_HEADS * (_nbytes((ts, ts), F32) + _nbytes((ts, ts), BF16) + 4 * _nbytes((ts, dh), BF16))
                   + _nbytes((ts, LANES), F32) + _nbytes((ts, w), F32))
    blocks = (2 * _nbytes((ts, 2 * w), BF16) + _nbytes((ts, SMALL_W), F32) + _nbytes((8, 3 * w), F32)
              + _nbytes((ts, w), BF16) + _nbytes(mk.shape, BF16) + _nbytes(mf.shape, F32))
    scratch = (2 * _nbytes((DN_HALO + ts, 3 * w), F32) + _nbytes((DN_HEADS, dh, dh), F32)
               + DN_HEADS * _nbytes((ts, 4 * dh), F32) + 2 * stage_bytes)
    return pl.pallas_call(
        _deltanet_kernel,
        grid=(B, ns + 1),
        in_specs=[
            pl.BlockSpec((ts, 2 * w), lambda b, s: (in_row(b, s), 2)),
            pl.BlockSpec((ts, 2 * w), lambda b, s: (in_row(b, s), 3)),
            pl.BlockSpec((ts, SMALL_W), lambda b, s: (in_row(b, s), 0)),
            const(DN_CONV_K, 3 * w), const(1, LANES), const(1, LANES), const(1, dh),
            const(*mk.shape), const(*mf.shape),
        ],
        out_specs=pl.BlockSpec((ts, w), lambda b, s: (out_row(b, s), 0)),
        out_shape=jax.ShapeDtypeStruct((B * S, w), BF16),
        scratch_shapes=[
            pltpu.VMEM((DN_HALO + ts, 3 * w), F32),
            pltpu.VMEM((ts, 3 * w), F32),
            pltpu.VMEM((DN_HEADS, dh, dh), F32),
            pltpu.VMEM((DN_HEADS, ts, 2 * dh), F32),
            pltpu.VMEM((DN_HEADS, ts, dh), BF16),
            pltpu.VMEM((DN_HEADS, ts, dh), F32),
        ] + stage + stage,
        compiler_params=pltpu.CompilerParams(
            dimension_semantics=("parallel", "arbitrary"),
            vmem_limit_bytes=_vmem_limit(blocks, scratch, 24 * _nbytes((ts, ts), F32)
                                         + 4 * _nbytes((ts, 3 * w), F32))),
        name="deltanet",
    )(z, z, sm, cw, alog, dtb, ng, mk, mf)


def _merge_kernel(x_ref, acp_ref, dn_ref, g0_ref, g1_ref, g2_ref, g3_ref, wb_ref, wo_ref, out_ref):
    w = MIX_W
    branches = (acp_ref[:, 0:w], dn_ref[...], acp_ref[:, w:2 * w], acp_ref[:, 2 * w:3 * w])
    gates = (g0_ref, g1_ref, g2_ref, g3_ref)
    merged = None
    for n in range(N_BRANCH):
        term = _sigmoid(gates[n][...].astype(F32)) * _dot(branches[n], wb_ref[n])
        merged = term if merged is None else merged + term
    out_ref[...] = x_ref[...] + _dot(merged.astype(BF16), wo_ref[...])


def _merge(x, acp, dn, z, wb, wo):
    T, D = x.shape
    w = MIX_W
    bm = min(512, T)
    gate_blk0 = (4 * 2 * w) // D
    gate_spec = lambda n: pl.BlockSpec((bm, D), lambda i: (i, gate_blk0 + n))
    blocks = (2 * _nbytes((bm, D), F32) + _nbytes((bm, 4 * w), BF16) + 4 * _nbytes((bm, D), BF16)
              + _nbytes((N_BRANCH, w, D), BF16) + _nbytes((D, D), BF16))
    return pl.pallas_call(
        _merge_kernel,
        grid=(T // bm,),
        in_specs=[
            pl.BlockSpec((bm, D), lambda i: (i, 0)),
            pl.BlockSpec((bm, 3 * w), lambda i: (i, 0)),
            pl.BlockSpec((bm, w), lambda i: (i, 0)),
            gate_spec(0), gate_spec(1), gate_spec(2), gate_spec(3),
            pl.BlockSpec((N_BRANCH, w, D), lambda i: (0, 0, 0)),
            pl.BlockSpec((D, D), lambda i: (0, 0)),
        ],
        out_specs=pl.BlockSpec((bm, D), lambda i: (i, 0)),
        out_shape=jax.ShapeDtypeStruct((T, D), F32),
        compiler_params=pltpu.CompilerParams(
            dimension_semantics=("parallel",),
            vmem_limit_bytes=_vmem_limit(blocks, 0, 4 * _nbytes((bm, D), F32))),
        name="merge",
    )(x, acp, dn, z, z, z, z, wb, wo)


def _xattn_kernel(x_ref, g_ref, wq_ref, kv_ref, wo_ref, out_ref):
    x = x_ref[...]
    q = _dot(_rmsnorm(x, g_ref[...]).astype(BF16), wq_ref[...])
    hd = XA_HEAD_DIM
    outs = []
    for h in range(XA_HEADS):
        qh = q[:, h * hd:(h + 1) * hd].astype(BF16)
        kh = kv_ref[:, h * hd:(h + 1) * hd]
        vh = kv_ref[:, D_MODEL + h * hd:D_MODEL + (h + 1) * hd]
        sc = _dot_nt(qh, kh) * np.float32(hd ** -0.5)
        p = jnp.exp(sc - jnp.max(sc, axis=-1, keepdims=True))
        p = p / jnp.sum(p, axis=-1, keepdims=True)
        outs.append(_dot(p.astype(BF16), vh).astype(BF16))
    out_ref[...] = x + _dot(jnp.concatenate(outs, axis=1), wo_ref[...])


def _xattn(x, g, wq, kv, wo, S):
    T, D = x.shape
    M = kv.shape[0] // (T // S)
    bm = min(512, S)
    per_b = S // bm
    blocks = (2 * _nbytes((bm, D), F32) + 2 * _nbytes((D, D), BF16) + _nbytes((M, 2 * D), BF16))
    return pl.pallas_call(
        _xattn_kernel,
        grid=(T // bm,),
        in_specs=[
            pl.BlockSpec((bm, D), lambda i: (i, 0)),
            pl.BlockSpec((1, D), lambda i: (0, 0)),
            pl.BlockSpec((D, D), lambda i: (0, 0)),
            pl.BlockSpec((M, 2 * D), lambda i: (i // per_b, 0)),
            pl.BlockSpec((D, D), lambda i: (0, 0)),
        ],
        out_specs=pl.BlockSpec((bm, D), lambda i: (i, 0)),
        out_shape=jax.ShapeDtypeStruct((T, D), F32),
        compiler_params=pltpu.CompilerParams(
            dimension_semantics=("parallel",),
            vmem_limit_bytes=_vmem_limit(blocks, 0, 6 * _nbytes((bm, D), F32))),
        name="xattn",
    )(x, g, wq, kv, wo)


def _mlp_kernel(x_ref, g_ref, w1_ref, w2_ref, gf_ref, out_ref, h_ref, *, final_norm):
    f = pl.program_id(1)

    @pl.when(f == 0)
    def _():
        x = x_ref[...]
        h_ref[...] = _rmsnorm(x, g_ref[...]).astype(BF16)
        out_ref[...] = x

    t = jnp.maximum(_dot(h_ref[...], w1_ref[...]), 0.0)
    out_ref[...] += _dot((t * t).astype(BF16), w2_ref[...])

    if final_norm:
        @pl.when(f == pl.num_programs(1) - 1)
        def _():
            out_ref[...] = _rmsnorm(out_ref[...], gf_ref[...])


def _mlp(x, g, w1, w2, gf, final_norm):
    T, D = x.shape
    F = w1.shape[1]
    bm = min(1024, T)
    bf = 1024
    blocks = (2 * _nbytes((bm, D), F32) + 2 * _nbytes((D, bf), BF16))
    return pl.pallas_call(
        functools.partial(_mlp_kernel, final_norm=final_norm),
        grid=(T // bm, F // bf),
        in_specs=[
            pl.BlockSpec((bm, D), lambda i, f: (i, 0)),
            pl.BlockSpec((1, D), lambda i, f: (0, 0)),
            pl.BlockSpec((D, bf), lambda i, f: (0, f)),
            pl.BlockSpec((bf, D), lambda i, f: (f, 0)),
            pl.BlockSpec((1, D), lambda i, f: (0, 0)),
        ],
        out_specs=pl.BlockSpec((bm, D), lambda i, f: (i, 0)),
        out_shape=jax.ShapeDtypeStruct((T, D), F32),
        scratch_shapes=[pltpu.VMEM((bm, D), BF16)],
        compiler_params=pltpu.CompilerParams(
            dimension_semantics=("parallel", "arbitrary"),
            vmem_limit_bytes=_vmem_limit(blocks, _nbytes((bm, D), BF16), 3 * _nbytes((bm, bf), F32))),
        name="mlp",
    )(x, g, w1, w2, gf)


def _permute_w_in(w_in):
    w = MIX_W
    sizes = [2 * w, w, w, w, w, DN_HEADS, DN_HEADS, 2 * w, w, N_BRANCH * D_MODEL]
    idx = np.cumsum([0] + sizes)
    seg = lambda i: w_in[:, idx[i]:idx[i + 1]]
    a_in, dq, dk, dv, dgate, dbeta, da, gm_in, pool_in, gate_in = (seg(i) for i in range(10))
    main = jnp.concatenate([a_in, gm_in, dq, dk, dv, dgate, gate_in, pool_in], axis=1).astype(BF16)
    pad = jnp.zeros((w_in.shape[0], LANES - DN_HEADS), w_in.dtype)
    small = jnp.concatenate([dbeta, pad, da, pad], axis=1).astype(BF16)
    return main, small


def _pad_lanes(v):
    return jnp.pad(v.astype(F32), (0, LANES - v.shape[0])).reshape(1, LANES)


def kernel(x, mem, norm_mix, w_in, conv_a_w, conv_a_b, ln_a_g, ln_a_b, dn_conv_w, dn_a_log, dn_dt_bias, dn_norm_g, gm_ln_g, gm_ln_b, gm_ws, gm_bs, pool_w, pool_scale, w_branch, w_out, norm_xa, norm_mem, xa_wq, xa_wkv, xa_wo, norm_mlp, mlp_w1, mlp_w2, norm_f):
    B, S, D = x.shape
    depth = w_in.shape[0]
    assert D == D_MODEL and S % DN_CHUNK == 0 and S % GM_CHUNK == 0
    T = B * S
    xf = x.reshape(T, D).astype(F32)
    memf = mem.reshape(B * mem.shape[1], D).astype(F32)
    row = lambda v: v.astype(F32).reshape(1, -1)

    for l in range(depth):
        w_main, w_small = _permute_w_in(w_in[l])
        z, sm = _inproj(xf, row(norm_mix[l]), w_main, w_small)
        bs_full = jnp.repeat(gm_bs[l].astype(F32).T, MIX_W // GM_GROUPS, axis=1)
        acp = _local_mixers(z, B, S, conv_a_w[l].astype(F32), row(conv_a_b[l]), row(ln_a_g[l]),
                            row(ln_a_b[l]), row(gm_ln_g[l]), row(gm_ln_b[l]), gm_ws[l].astype(F32),
                            bs_full, pool_w[l].astype(BF16), row(pool_scale[l]))
        dn = _deltanet(z, sm, B, S, dn_conv_w[l].astype(F32), _pad_lanes(dn_a_log[l]),
                       _pad_lanes(dn_dt_bias[l]), row(dn_norm_g[l]))
        xf = _merge(xf, acp, dn, z, w_branch[l].astype(BF16), w_out[l].astype(BF16))

        kv = _rms_matmul(memf, row(norm_mem[l]), xa_wkv[l].astype(BF16), 1024)
        xf = _xattn(xf, row(norm_xa[l]), xa_wq[l].astype(BF16), kv, xa_wo[l].astype(BF16), S)

        xf = _mlp(xf, row(norm_mlp[l]), mlp_w1[l].astype(BF16), mlp_w2[l].astype(BF16),
                  row(norm_f), final_norm=(l == depth - 1))
    return xf.reshape(B, S, D).astype(x.dtype)
```

```python
import functools

import jax
import jax.numpy as jnp
import numpy as np
from jax import lax
from jax.experimental import pallas as pl
from jax.experimental.pallas import tpu as pltpu

F32 = jnp.float32
BF16 = jnp.bfloat16

D_MODEL = 1024
MIX_W = D_MODEL // 2
N_BRANCH = 4
CONV_K = 31
DN_HEADS = 4
DN_HEAD_DIM = MIX_W // DN_HEADS
DN_CONV_K = 4
DN_CHUNK = 64
GM_GROUPS = 4
GM_CHUNK = 128
POOL_WINDOWS = (2, 4, 8, 16)
XA_HEADS = 4
XA_HEAD_DIM = D_MODEL // XA_HEADS
FFN_W = 4 * D_MODEL

V7X_VMEM_BYTES = 64 * 1024 * 1024
LANES = 128
SUBLANES = 8

ZL_W = 2 * MIX_W + 2 * MIX_W + MIX_W
ZR_W = 4 * MIX_W + N_BRANCH * D_MODEL
SMALL_W = 2 * LANES
IN_BM = 1024
IN_STEPS = 4
ZR_BN = ZR_W // IN_STEPS
IN_DOT = 512
LOCAL_PIECES_PER_DOT = 8

CONV_HALO = 32
POOL_HALO = 32
DN_HALO = 8
LOCAL_TS = IN_BM // IN_STEPS
DN_TS = 256
DN_DIAG = 128


def _vmem_limit(block_bytes, scratch_bytes=0, temp_bytes=0):
    need = 2 * block_bytes + scratch_bytes + temp_bytes + (4 << 20)
    return int(min(need, V7X_VMEM_BYTES - (6 << 20)))


def _nbytes(shape, dtype):
    return int(np.prod(shape)) * jnp.dtype(dtype).itemsize


def _rmsnorm(x, g, eps=1e-6):
    return x * lax.rsqrt(jnp.mean(x * x, axis=-1, keepdims=True) + eps) * g


def _layernorm(x, g, b, eps=1e-5):
    mu = jnp.mean(x, axis=-1, keepdims=True)
    xc = x - mu
    var = jnp.mean(xc * xc, axis=-1, keepdims=True)
    return xc * lax.rsqrt(var + eps) * g + b


def _sigmoid(x):
    return 0.5 * jnp.tanh(0.5 * x) + 0.5


def _swish(x):
    h = 0.5 * x
    return h * jnp.tanh(h) + h


def _dot(a, b):
    return jnp.dot(a, b, preferred_element_type=F32)


def _dot_nt(a, b):
    return lax.dot_general(a, b, (((1,), (1,)), ((), ())), preferred_element_type=F32)


def _dot_tn(a, b):
    return lax.dot_general(a, b, (((0,), (0,)), ((), ())), preferred_element_type=F32)


def _interleave(*weighted):
    live = list(weighted)
    while live:
        for item in list(live):
            gen, n = item
            for _ in range(n):
                if next(gen, StopIteration) is StopIteration:
                    live.remove(item)
                    break


def _proj_columns(h_ref, w_ref, z_ref):
    for c0 in range(0, w_ref.shape[1], IN_DOT):
        z_ref[:, c0:c0 + IN_DOT] = _dot(h_ref[...], w_ref[:, c0:c0 + IN_DOT]).astype(z_ref.dtype)
        yield


def _inproj_local_kernel(x_ref, g_ref, wl_ref, wr_ref, ws_ref, cw_ref, cb_ref, lag_ref, lab_ref,
                         glg_ref, glb_ref, gws_ref, bs_ref, pw_ref, ps_ref,
                         zr_ref, sm_ref, acp_ref,
                         h_ref, zl_ref, zt_ref, abuf, ashift, cwb, ybuf, pbuf, tbuf, *, tiles_per_seq):
    i = pl.program_id(0)
    j = pl.program_id(1)

    @pl.when(j == 0)
    def _():
        h_ref[...] = _rmsnorm(x_ref[...], g_ref[...]).astype(BF16)
        sm_ref[...] = _dot(h_ref[...], ws_ref[...])
        for _ in _proj_columns(h_ref, wl_ref, zl_ref):
            pass

    @pl.when(j > 0)
    def _():
        sub = j - 1
        tile = (i % tiles_per_seq) * IN_STEPS + sub
        zt_ref[...] = zl_ref[pl.ds(pl.multiple_of(sub * LOCAL_TS, LOCAL_TS), LOCAL_TS), :]
        w = MIX_W
        local = _local_tile(
            zt_ref.at[:, pl.ds(0, 2 * w)], zt_ref.at[:, pl.ds(2 * w, 2 * w)],
            zt_ref.at[:, pl.ds(4 * w, w)], cw_ref, cb_ref, lag_ref, lab_ref, glg_ref, glb_ref,
            gws_ref, bs_ref, pw_ref, ps_ref, acp_ref, abuf, ashift, cwb, ybuf, pbuf, tbuf, tile)
        next(local)
        _interleave((_proj_columns(h_ref, wr_ref, zr_ref), 1), (local, LOCAL_PIECES_PER_DOT))


def _inproj_local(x, S, g, wl, wr, ws, cw, cb, lag, lab, glg, glb, gws, bs_full, pw, ps):
    T, D = x.shape
    bm = IN_BM
    assert S % bm == 0 and T % bm == 0
    w = MIX_W
    ts = LOCAL_TS
    const = lambda *shape: pl.BlockSpec(shape, lambda i, j: (0,) * len(shape))
    once = lambda *shape: pl.BlockSpec(shape, lambda i, j: (0,) * len(shape), pipeline_mode=pl.Buffered(1))
    prev = lambda j: jnp.maximum(j - 1, 0)
    blocks = (_nbytes((bm, D), F32) + _nbytes((D, ZR_BN), BF16) + _nbytes((D, SMALL_W), BF16)
              + _nbytes((bm, ZR_BN), BF16) + _nbytes((bm, SMALL_W), F32) + _nbytes((ts, 3 * w), BF16)
              + _nbytes((CONV_K + 8, w), F32) + _nbytes((GM_GROUPS, GM_CHUNK, GM_CHUNK), F32)
              + _nbytes((GM_CHUNK, w), F32) + _nbytes((4, 128, 128), BF16))
    scratch = (_nbytes((D, ZL_W), BF16) + _nbytes((bm, D), BF16) + _nbytes((bm, ZL_W), BF16)
               + SUBLANES * _nbytes((CONV_HALO + ts, w), F32) + _nbytes((ts, w), F32)
               + _nbytes((CONV_K + 1, SUBLANES, w), F32) + 2 * _nbytes((POOL_HALO + ts, w), F32))
    return pl.pallas_call(
        functools.partial(_inproj_local_kernel, tiles_per_seq=S // bm),
        grid=(T // bm, 1 + IN_STEPS),
        in_specs=[
            pl.BlockSpec((bm, D), lambda i, j: (i, 0)),
            const(1, D),
            once(D, ZL_W),
            pl.BlockSpec((D, ZR_BN), lambda i, j: (0, prev(j))),
            const(D, SMALL_W),
            const(CONV_K, w), const(1, w), const(1, w), const(1, w), const(1, w), const(1, w),
            const(GM_GROUPS, GM_CHUNK, GM_CHUNK), const(GM_CHUNK, w),
            const(len(POOL_WINDOWS), w // 4, w // 4), const(1, w),
        ],
        out_specs=[
            pl.BlockSpec((bm, ZR_BN), lambda i, j: (i, prev(j))),
            pl.BlockSpec((bm, SMALL_W), lambda i, j: (i, 0)),
            pl.BlockSpec((ts, 3 * w), lambda i, j: (i * IN_STEPS + prev(j), 0)),
        ],
        out_shape=[jax.ShapeDtypeStruct((T, ZR_W), BF16), jax.ShapeDtypeStruct((T, SMALL_W), F32),
                   jax.ShapeDtypeStruct((T, 3 * w), BF16)],
        scratch_shapes=[
            pltpu.VMEM((bm, D), BF16),
            pltpu.VMEM((bm, ZL_W), BF16),
            pltpu.VMEM((ts, ZL_W), BF16),
            pltpu.VMEM((CONV_HALO + ts, w), F32),
            pltpu.VMEM((SUBLANES - 1, CONV_HALO + ts - SUBLANES, w), F32),
            pltpu.VMEM((CONV_K + 1, SUBLANES, w), F32),
            pltpu.VMEM((ts, w), F32),
            pltpu.VMEM((POOL_HALO + ts, w), F32),
            pltpu.VMEM((POOL_HALO + ts, w), F32),
        ],
        compiler_params=pltpu.CompilerParams(
            dimension_semantics=("arbitrary", "arbitrary"),
            vmem_limit_bytes=_vmem_limit(blocks, scratch,
                                         _nbytes((bm, D), F32) + 2 * _nbytes((bm, IN_DOT), F32)
                                         + 8 * _nbytes((ts, 2 * w), F32))),
        name="inproj_local",
    )(x, g, wl, wr, ws, cw, cb, lag, lab, glg, glb, gws, bs_full, pw, ps)


def _rms_matmul_kernel(x_ref, g_ref, w_ref, o_ref, h_ref):
    @pl.when(pl.program_id(1) == 0)
    def _():
        h_ref[...] = _rmsnorm(x_ref[...], g_ref[...]).astype(BF16)

    o_ref[...] = _dot(h_ref[...], w_ref[...]).astype(o_ref.dtype)


def _rms_matmul(x, g, w, bn):
    T, D = x.shape
    N = w.shape[1]
    bm = min(1024, T)
    blocks = _nbytes((bm, D), F32) + _nbytes((D, bn), BF16) + _nbytes((bm, bn), BF16)
    return pl.pallas_call(
        _rms_matmul_kernel,
        grid=(T // bm, N // bn),
        in_specs=[
            pl.BlockSpec((bm, D), lambda i, j: (i, 0)),
            pl.BlockSpec((1, D), lambda i, j: (0, 0)),
            pl.BlockSpec((D, bn), lambda i, j: (0, j)),
        ],
        out_specs=pl.BlockSpec((bm, bn), lambda i, j: (i, j)),
        out_shape=jax.ShapeDtypeStruct((T, N), BF16),
        scratch_shapes=[pltpu.VMEM((bm, D), BF16)],
        compiler_params=pltpu.CompilerParams(
            dimension_semantics=("parallel", "arbitrary"),
            vmem_limit_bytes=_vmem_limit(blocks, _nbytes((bm, D), BF16), _nbytes((bm, D), F32))),
        name="mem_kv",
    )(x, g, w)


def _local_tile(ain_ref, gm_ref, pin_ref, cw_ref, cb_ref, lag_ref, lab_ref, glg_ref, glb_ref,
                ws_ref, bs_ref, pw_ref, ps_ref, out_ref, abuf, ashift, cwb, ybuf, pbuf, tbuf, tile):
    ts = ain_ref.shape[0]
    w = MIX_W

    @pl.when(tile == 0)
    def _():
        abuf[0:CONV_HALO, :] = jnp.zeros((CONV_HALO, w), F32)
        pbuf[0:POOL_HALO, :] = jnp.zeros((POOL_HALO, w), F32)
        for k in range(CONV_K):
            cwb[k] = jnp.broadcast_to(cw_ref[k:k + 1, :], (SUBLANES, w))
        cwb[CONV_K] = jnp.broadcast_to(cb_ref[...], (SUBLANES, w))
    yield

    ain = ain_ref[...].astype(F32)
    abuf[CONV_HALO:CONV_HALO + ts, :] = ain[:, :w] * _sigmoid(ain[:, w:])
    yield
    n_sh = CONV_HALO + ts - SUBLANES
    for r in range(1, SUBLANES):
        ashift[r - 1] = abuf[r:r + n_sh, :]
        if r % 2 == 0:
            yield
    rb = 4 * SUBLANES
    for r0 in range(0, ts, rb):
        acc = [cwb[CONV_K]] * (rb // SUBLANES)
        for k in range(CONV_K):
            off = CONV_HALO - (CONV_K - 1) + k + r0
            r = off % SUBLANES
            wk = cwb[k]
            for q in range(rb // SUBLANES):
                lo = off - r + q * SUBLANES
                src = abuf[lo:lo + SUBLANES, :] if r == 0 else ashift[r - 1, lo:lo + SUBLANES, :]
                acc[q] = acc[q] + wk * src
        for q in range(rb // SUBLANES):
            ybuf[r0 + q * SUBLANES:r0 + (q + 1) * SUBLANES, :] = acc[q]
        yield
    abuf[0:CONV_HALO, :] = abuf[ts:ts + CONV_HALO, :]
    ya = _layernorm(ybuf[...], lag_ref[...], lab_ref[...])
    out_ref[:, 0:w] = _swish(ya).astype(out_ref.dtype)
    yield

    gm = gm_ref[...].astype(F32)
    ge = 0.5 * gm * (1.0 + lax.erf(gm * np.float32(np.sqrt(0.5))))
    yield
    ybuf[...] = _layernorm(ge[:, w:], glg_ref[...], glb_ref[...])
    yield
    gw = w // GM_GROUPS
    row = lax.broadcasted_iota(jnp.int32, (GM_CHUNK, GM_CHUNK), 0)
    col = lax.broadcasted_iota(jnp.int32, (GM_CHUNK, GM_CHUNK), 1)
    for g in range(GM_GROUPS):
        wsg = jnp.where(row >= col, ws_ref[g], 0.0).astype(BF16)
        for n in range(ts // GM_CHUNK):
            rows = slice(n * GM_CHUNK, (n + 1) * GM_CHUNK)
            cols = slice(g * gw, (g + 1) * gw)
            mixed = _dot(wsg, ybuf[rows, cols].astype(BF16)) + bs_ref[:, cols]
            out_ref[rows, w + g * gw:w + (g + 1) * gw] = (ge[rows, cols] * mixed).astype(out_ref.dtype)
        yield

    pin = pin_ref[...].astype(F32)
    pbuf[POOL_HALO:POOL_HALO + ts, :] = pin
    tpos = tile * ts + lax.broadcasted_iota(jnp.int32, (ts, 1), 0)
    n_ext = POOL_HALO + ts
    pw_cols = w // len(POOL_WINDOWS)
    for gi, win in enumerate(POOL_WINDOWS):
        cols = slice(gi * pw_cols, (gi + 1) * pw_cols)
        src = pbuf
        d = 1
        lo = SUBLANES
        while d < win:
            tbuf[lo:n_ext, cols] = src[lo:n_ext, cols] + src[lo - d:n_ext - d, cols]
            src = tbuf
            d *= 2
            lo += SUBLANES
        wsum = src[POOL_HALO:n_ext, cols]
        inv = 1.0 / jnp.minimum(tpos + 1, win).astype(F32)
        pg = (wsum * inv - pin[:, cols]).astype(BF16)
        lin = _dot(pg, pw_ref[gi]) * ps_ref[:, cols]
        out_ref[:, 2 * w + gi * pw_cols:2 * w + (gi + 1) * pw_cols] = lin.astype(out_ref.dtype)
        yield
    pbuf[0:POOL_HALO, :] = pbuf[ts:ts + POOL_HALO, :]


def _split3(x):
    hi = x.astype(BF16)
    r1 = x - hi.astype(F32)
    mid = r1.astype(BF16)
    lo = (r1 - mid.astype(F32)).astype(BF16)
    return hi, mid, lo


def _dot_mask(mask_bf16, x):
    hi, mid, lo = _split3(x)
    return _dot(mask_bf16, hi) + _dot(mask_bf16, mid) + _dot(mask_bf16, lo)


def _deltanet_masks(ts):
    def causal_same(n):
        row, col = np.indices((n, n))
        same = (row // DN_CHUNK) == (col // DN_CHUNK)
        return same & (row >= col), same, row, col

    causal, same, _, _ = causal_same(ts)
    mk = np.stack([causal, same]).astype(np.float32)
    causal, _, row, col = causal_same(DN_DIAG)
    levels = []
    m = 1
    while m < DN_CHUNK:
        levels.append(((row // (2 * m)) == (col // (2 * m))) & ((row & m) != 0) & ((col & m) == 0))
        m *= 2
    mf = np.stack([causal] + levels + [row == col]).astype(np.float32)
    return jnp.asarray(mk, BF16), jnp.asarray(mf, F32)


def _blockdiag_bf16(squares):
    n = len(squares)
    zero = jnp.zeros(squares[0].shape, BF16)
    rows = [jnp.concatenate([squares[i].astype(BF16) if j == i else zero for j in range(n)], axis=1)
            for i in range(n)]
    return jnp.concatenate(rows, axis=0)


def _deltanet_front(qk_ref, vd_ref, sm_ref, cw_ref, alog_ref, dtb_ref, mk_ref, mf_ref, xbuf, qkv_s,
                    stage):
    a_s, attn_s, rhs_s, qdec_s, kdec_s, egl_s, gate_s = stage
    ts = qk_ref.shape[0]
    w = MIX_W
    dh = DN_HEAD_DIM

    for cg in range(3 * w // LANES):
        cols = slice(cg * LANES, (cg + 1) * LANES)
        src = qk_ref[:, cols] if cg < 2 * w // LANES else vd_ref[:, cg * LANES - 2 * w:(cg + 1) * LANES - 2 * w]
        xbuf[DN_HALO:DN_HALO + ts, cols] = src.astype(F32)
        y = None
        for k in range(DN_CONV_K):
            off = DN_HALO - (DN_CONV_K - 1) + k
            tap = cw_ref[k:k + 1, cols] * xbuf[off:off + ts, cols]
            y = tap if y is None else y + tap
        xbuf[0:DN_HALO, cols] = xbuf[ts:ts + DN_HALO, cols]
        qkv_s[:, cols] = _swish(y)
        if cg % 2 == 1:
            yield

    sm = sm_ref[...]
    beta = _sigmoid(sm[:, 0:LANES])
    da = sm[:, LANES:2 * LANES] + dtb_ref[...]
    softplus = jnp.maximum(da, 0.0) + jnp.log1p(jnp.exp(-jnp.abs(da)))
    g = -jnp.exp(alog_ref[...]) * softplus

    gam = _dot_mask(mk_ref[0], g)
    glast = _dot_mask(mk_ref[1], g)
    gam_t = gam.T
    causal = mf_ref[0]
    egl_s[...] = jnp.exp(glast)
    yield
    for h in range(DN_HEADS):
        dg = vd_ref[:, w + h * dh:w + (h + 1) * dh].astype(F32)
        gate_s[:, h * dh:(h + 1) * dh] = _swish(dg)
        qh = qkv_s[:, h * dh:(h + 1) * dh]
        kh = qkv_s[:, w + h * dh:w + (h + 1) * dh]
        vh = qkv_s[:, 2 * w + h * dh:2 * w + (h + 1) * dh]
        qn = qh * lax.rsqrt(jnp.sum(qh * qh, axis=-1, keepdims=True) + 1e-6) * np.float32(dh ** -0.5)
        kn = kh * lax.rsqrt(jnp.sum(kh * kh, axis=-1, keepdims=True) + 1e-6)
        bcol = beta[:, h:h + 1]
        gcol = gam[:, h:h + 1]
        kb = kn * bcol
        kn16 = kn.astype(BF16)
        kk = _dot_nt(kb.astype(BF16), kn16)
        qk = _dot_nt(qn.astype(BF16), kn16)
        for d in range(ts // DN_DIAG):
            dd = slice(d * DN_DIAG, (d + 1) * DN_DIAG)
            decay = jnp.exp((gcol[dd] - gam_t[h:h + 1, dd]) * causal) * causal
            a_s[h, d] = kk[dd, dd] * decay
            attn_s[h, dd, dd] = (qk[dd, dd] * decay).astype(BF16)
        eg = jnp.exp(gcol)
        rhs_s[h] = jnp.concatenate([vh * bcol, kb * eg], axis=1).astype(BF16)
        qdec_s[h] = (qn * eg).astype(BF16)
        kdec_s[h] = (kn * jnp.exp(glast[:, h:h + 1] - gcol)).astype(BF16)
        yield


def _deltanet_back(ng_ref, mf_ref, o_ref, stage, state, sol_s, vnew_s, qs_s):
    a_s, attn_s, rhs_s, qdec_s, kdec_s, egl_s, gate_s = stage
    ts = o_ref.shape[0]
    dh = DN_HEAD_DIM
    ck = DN_CHUNK
    heads = range(DN_HEADS)

    n_lvl = ck.bit_length() - 1
    diags = range(ts // DN_DIAG)
    sq = lambda m, d: m[d * DN_DIAG:(d + 1) * DN_DIAG, d * DN_DIAG:(d + 1) * DN_DIAG]
    x = [[mf_ref[1 + n_lvl] - mf_ref[1] * a_s[h, d] for d in diags] for h in heads]
    for lvl in range(1, n_lvl):
        x16 = [_blockdiag_bf16(x[h]) for h in heads]
        b16 = [_blockdiag_bf16([mf_ref[1 + lvl] * a_s[h, d] for d in diags]) for h in heads]
        y = [_dot(b16[h], x16[h]) for h in heads]
        y16 = [_blockdiag_bf16([sq(y[h], d) for d in diags]) for h in heads]
        xy = [_dot(x16[h], y16[h]) for h in heads]
        x = [[x[h][d] - sq(xy[h], d) for d in diags] for h in heads]
        yield
    for h in heads:
        sol_s[h] = _dot(_blockdiag_bf16(x[h]), rhs_s[h])
    yield

    st = [state[h] for h in heads]
    for c in range(ts // ck):
        rows = slice(c * ck, (c + 1) * ck)
        for h in heads:
            wq = jnp.concatenate([sol_s[h, rows, dh:2 * dh].astype(BF16), qdec_s[h, rows, :]], axis=0)
            r = _dot(wq, st[h].astype(BF16))
            vnew = sol_s[h, rows, 0:dh] - r[0:ck, :]
            vnew_s[h, rows, :] = vnew.astype(BF16)
            qs_s[h, rows, :] = r[ck:2 * ck, :]
            scale = egl_s[c * ck:c * ck + 1, h:h + 1]
            st[h] = st[h] * scale + _dot_tn(kdec_s[h, rows, :], vnew.astype(BF16))
        yield
    for h in heads:
        state[h] = st[h]
        o = qs_s[h] + _dot(attn_s[h], vnew_s[h])
        on = _rmsnorm(o, ng_ref[...])
        o_ref[:, h * dh:(h + 1) * dh] = (on * gate_s[:, h * dh:(h + 1) * dh]).astype(o_ref.dtype)


def _deltanet_kernel(qk_ref, vd_ref, sm_ref, cw_ref, alog_ref, dtb_ref, ng_ref, mk_ref, mf_ref,
                     o_ref, xbuf, qkv_s, state, sol_s, vnew_s, qs_s, *stages):
    s = pl.program_id(1)
    n_stage = len(stages) // 2
    stage_a, stage_b = stages[:n_stage], stages[n_stage:]

    @pl.when(s == 0)
    def _():
        xbuf[0:DN_HALO, :] = jnp.zeros((DN_HALO, xbuf.shape[1]), F32)
        state[...] = jnp.zeros(state.shape, F32)
        for ref in stage_a + stage_b:
            ref[...] = jnp.zeros(ref.shape, ref.dtype)

    def step(write_stage, read_stage):
        halves = [
            _deltanet_back(ng_ref, mf_ref, o_ref, read_stage, state, sol_s, vnew_s, qs_s),
            _deltanet_front(qk_ref, vd_ref, sm_ref, cw_ref, alog_ref, dtb_ref, mk_ref, mf_ref, xbuf,
                            qkv_s, write_stage),
        ]
        while halves:
            for half in list(halves):
                if next(half, StopIteration) is StopIteration:
                    halves.remove(half)

    @pl.when(s % 2 == 0)
    def _():
        step(stage_a, stage_b)

    @pl.when(s % 2 == 1)
    def _():
        step(stage_b, stage_a)


def _deltanet(z, sm, B, S, cw, alog, dtb, ng):
    ts = min(DN_TS, S)
    ns = S // ts
    w = MIX_W
    dh = DN_HEAD_DIM
    const = lambda *shape: pl.BlockSpec(shape, lambda b, s: (0,) * len(shape))
    mk, mf = _deltanet_masks(ts)
    in_row = lambda b, s: b * ns + jnp.minimum(s, ns - 1)
    out_row = lambda b, s: b * ns + jnp.maximum(s - 1, 0)
    stage = [
        pltpu.VMEM((DN_HEADS, ts // DN_DIAG, DN_DIAG, DN_DIAG), F32),
        pltpu.VMEM((DN_HEADS, ts, ts), BF16),
        pltpu.VMEM((DN_HEADS, ts, 2 * dh), BF16),
        pltpu.VMEM((DN_HEADS, ts, dh), BF16),
        pltpu.VMEM((DN_HEADS, ts, dh), BF16),
        pltpu.VMEM((ts, LANES), F32),
        pltpu.VMEM((ts, w), F32),
    ]
    stage_bytes = (DN_HEADS * (_nbytes((ts, ts), F32) + _nbytes((ts, ts), BF16) + 4 * _nbytes((ts, dh), BF16))
                   + _nbytes((ts, LANES), F32) + _nbytes((ts, w), F32))
    blocks = (2 * _nbytes((ts, 2 * w), BF16) + _nbytes((ts, SMALL_W), F32) + _nbytes((8, 3 * w), F32)
              + _nbytes((ts, w), BF16) + _nbytes(mk.shape, BF16) + _nbytes(mf.shape, F32))
    scratch = (2 * _nbytes((DN_HALO + ts, 3 * w), F32) + _nbytes((DN_HEADS, dh, dh), F32)
               + DN_HEADS * _nbytes((ts, 4 * dh), F32) + 2 * stage_bytes)
    return pl.pallas_call(
        _deltanet_kernel,
        grid=(B, ns + 1),
        in_specs=[
            pl.BlockSpec((ts, 2 * w), lambda b, s: (in_row(b, s), 0)),
            pl.BlockSpec((ts, 2 * w), lambda b, s: (in_row(b, s), 1)),
            pl.BlockSpec((ts, SMALL_W), lambda b, s: (in_row(b, s), 0)),
            const(DN_CONV_K, 3 * w), const(1, LANES), const(1, LANES), const(1, dh),
            const(*mk.shape), const(*mf.shape),
        ],
        out_specs=pl.BlockSpec((ts, w), lambda b, s: (out_row(b, s), 0)),
        out_shape=jax.ShapeDtypeStruct((B * S, w), BF16),
        scratch_shapes=[
            pltpu.VMEM((DN_HALO + ts, 3 * w), F32),
            pltpu.VMEM((ts, 3 * w), F32),
            pltpu.VMEM((DN_HEADS, dh, dh), F32),
            pltpu.VMEM((DN_HEADS, ts, 2 * dh), F32),
            pltpu.VMEM((DN_HEADS, ts, dh), BF16),
            pltpu.VMEM((DN_HEADS, ts, dh), F32),
        ] + stage + stage,
        compiler_params=pltpu.CompilerParams(
            dimension_semantics=("parallel", "arbitrary"),
            vmem_limit_bytes=_vmem_limit(blocks, scratch, 24 * _nbytes((ts, ts), F32)
                                         + 4 * _nbytes((ts, 3 * w), F32))),
        name="deltanet",
    )(z, z, sm, cw, alog, dtb, ng, mk, mf)


def _merge_kernel(x_ref, acp_ref, dn_ref, g0_ref, g1_ref, g2_ref, g3_ref, wb_ref, wo_ref, out_ref):
    w = MIX_W
    branches = (acp_ref[:, 0:w], dn_ref[...], acp_ref[:, w:2 * w], acp_ref[:, 2 * w:3 * w])
    gates = (g0_ref, g1_ref, g2_ref, g3_ref)
    merged = None
    for n in range(N_BRANCH):
        term = _sigmoid(gates[n][...].astype(F32)) * _dot(branches[n], wb_ref[n])
        merged = term if merged is None else merged + term
    out_ref[...] = x_ref[...] + _dot(merged.astype(BF16), wo_ref[...])


def _merge(x, acp, dn, z, wb, wo):
    T, D = x.shape
    w = MIX_W
    bm = min(512, T)
    gate_blk0 = (4 * w) // D
    gate_spec = lambda n: pl.BlockSpec((bm, D), lambda i: (i, gate_blk0 + n))
    blocks = (2 * _nbytes((bm, D), F32) + _nbytes((bm, 4 * w), BF16) + 4 * _nbytes((bm, D), BF16)
              + _nbytes((N_BRANCH, w, D), BF16) + _nbytes((D, D), BF16))
    return pl.pallas_call(
        _merge_kernel,
        grid=(T // bm,),
        in_specs=[
            pl.BlockSpec((bm, D), lambda i: (i, 0)),
            pl.BlockSpec((bm, 3 * w), lambda i: (i, 0)),
            pl.BlockSpec((bm, w), lambda i: (i, 0)),
            gate_spec(0), gate_spec(1), gate_spec(2), gate_spec(3),
            pl.BlockSpec((N_BRANCH, w, D), lambda i: (0, 0, 0)),
            pl.BlockSpec((D, D), lambda i: (0, 0)),
        ],
        out_specs=pl.BlockSpec((bm, D), lambda i: (i, 0)),
        out_shape=jax.ShapeDtypeStruct((T, D), F32),
        compiler_params=pltpu.CompilerParams(
            dimension_semantics=("parallel",),
            vmem_limit_bytes=_vmem_limit(blocks, 0, 4 * _nbytes((bm, D), F32))),
        name="merge",
    )(x, acp, dn, z, z, z, z, wb, wo)


def _xattn_kernel(x_ref, g_ref, wq_ref, kv_ref, wo_ref, out_ref):
    x = x_ref[...]
    hn = _rmsnorm(x, g_ref[...]).astype(BF16)
    hd = XA_HEAD_DIM
    heads = range(XA_HEADS)
    q = _dot(hn, wq_ref[...])
    sc, p, o = {}, {}, {}

    def stage(h, t):
        cols = slice(h * hd, (h + 1) * hd)
        if t == 0:
            sc[h] = _dot_nt(q[:, cols].astype(BF16), kv_ref[:, cols]) * np.float32(hd ** -0.5)
        elif t == 1:
            e = jnp.exp(sc[h] - jnp.max(sc[h], axis=-1, keepdims=True))
            p[h] = (e / jnp.sum(e, axis=-1, keepdims=True)).astype(BF16)
        else:
            o[h] = _dot(p[h], kv_ref[:, D_MODEL + h * hd:D_MODEL + (h + 1) * hd]).astype(BF16)

    n_stage = 3
    for t in range(n_stage + XA_HEADS - 1):
        for h in heads:
            if 0 <= t - h < n_stage:
                stage(h, t - h)
    out_ref[...] = x + _dot(jnp.concatenate([o[h] for h in heads], axis=1), wo_ref[...])


def _xattn(x, g, wq, kv, wo, S):
    T, D = x.shape
    M = kv.shape[0] // (T // S)
    bm = min(1024, S)
    per_b = S // bm
    blocks = (2 * _nbytes((bm, D), F32) + 2 * _nbytes((D, D), BF16) + _nbytes((M, 2 * D), BF16))
    return pl.pallas_call(
        _xattn_kernel,
        grid=(T // bm,),
        in_specs=[
            pl.BlockSpec((bm, D), lambda i: (i, 0)),
            pl.BlockSpec((1, D), lambda i: (0, 0)),
            pl.BlockSpec((D, D), lambda i: (0, 0)),
            pl.BlockSpec((M, 2 * D), lambda i: (i // per_b, 0)),
            pl.BlockSpec((D, D), lambda i: (0, 0)),
        ],
        out_specs=pl.BlockSpec((bm, D), lambda i: (i, 0)),
        out_shape=jax.ShapeDtypeStruct((T, D), F32),
        compiler_params=pltpu.CompilerParams(
            dimension_semantics=("parallel",),
            vmem_limit_bytes=_vmem_limit(blocks, 0, 6 * _nbytes((bm, D), F32))),
        name="xattn",
    )(x, g, wq, kv, wo)


def _mlp_kernel(x_ref, g_ref, w1_ref, w2_ref, gf_ref, out_ref, h_ref, *, final_norm):
    f = pl.program_id(1)

    @pl.when(f == 0)
    def _():
        x = x_ref[...]
        h_ref[...] = _rmsnorm(x, g_ref[...]).astype(BF16)
        out_ref[...] = x

    t = jnp.maximum(_dot(h_ref[...], w1_ref[...]), 0.0)
    out_ref[...] += _dot((t * t).astype(BF16), w2_ref[...])

    if final_norm:
        @pl.when(f == pl.num_programs(1) - 1)
        def _():
            out_ref[...] = _rmsnorm(out_ref[...], gf_ref[...])


def _mlp(x, g, w1, w2, gf, final_norm):
    T, D = x.shape
    F = w1.shape[1]
    bm = min(1024, T)
    bf = 1024
    blocks = (2 * _nbytes((bm, D), F32) + 2 * _nbytes((D, bf), BF16))
    return pl.pallas_call(
        functools.partial(_mlp_kernel, final_norm=final_norm),
        grid=(T // bm, F // bf),
        in_specs=[
            pl.BlockSpec((bm, D), lambda i, f: (i, 0)),
            pl.BlockSpec((1, D), lambda i, f: (0, 0)),
            pl.BlockSpec((D, bf), lambda i, f: (0, f)),
            pl.BlockSpec((bf, D), lambda i, f: (f, 0)),
            pl.BlockSpec((1, D), lambda i, f: (0, 0)),
        ],
        out_specs=pl.BlockSpec((bm, D), lambda i, f: (i, 0)),
        out_shape=jax.ShapeDtypeStruct((T, D), F32),
        scratch_shapes=[pltpu.VMEM((bm, D), BF16)],
        compiler_params=pltpu.CompilerParams(
            dimension_semantics=("parallel", "arbitrary"),
            vmem_limit_bytes=_vmem_limit(blocks, _nbytes((bm, D), BF16), 3 * _nbytes((bm, bf), F32))),
        name="mlp",
    )(x, g, w1, w2, gf)


def _permute_w_in(w_in):
    w = MIX_W
    sizes = [2 * w, w, w, w, w, DN_HEADS, DN_HEADS, 2 * w, w, N_BRANCH * D_MODEL]
    idx = np.cumsum([0] + sizes)
    seg = lambda i: w_in[:, idx[i]:idx[i + 1]]
    a_in, dq, dk, dv, dgate, dbeta, da, gm_in, pool_in, gate_in = (seg(i) for i in range(10))
    local = jnp.concatenate([a_in, gm_in, pool_in], axis=1).astype(BF16)
    rest = jnp.concatenate([dq, dk, dv, dgate, gate_in], axis=1).astype(BF16)
    pad = jnp.zeros((w_in.shape[0], LANES - DN_HEADS), w_in.dtype)
    small = jnp.concatenate([dbeta, pad, da, pad], axis=1).astype(BF16)
    return local, rest, small


def _pad_lanes(v):
    return jnp.pad(v.astype(F32), (0, LANES - v.shape[0])).reshape(1, LANES)


def kernel(x, mem, norm_mix, w_in, conv_a_w, conv_a_b, ln_a_g, ln_a_b, dn_conv_w, dn_a_log, dn_dt_bias, dn_norm_g, gm_ln_g, gm_ln_b, gm_ws, gm_bs, pool_w, pool_scale, w_branch, w_out, norm_xa, norm_mem, xa_wq, xa_wkv, xa_wo, norm_mlp, mlp_w1, mlp_w2, norm_f):
    B, S, D = x.shape
    depth = w_in.shape[0]
    assert D == D_MODEL and S % DN_CHUNK == 0 and S % GM_CHUNK == 0
    T = B * S
    xf = x.reshape(T, D).astype(F32)
    memf = mem.reshape(B * mem.shape[1], D).astype(F32)
    row = lambda v: v.astype(F32).reshape(1, -1)

    for l in range(depth):
        w_local, w_rest, w_small = _permute_w_in(w_in[l])
        bs_full = jnp.repeat(gm_bs[l].astype(F32).T, MIX_W // GM_GROUPS, axis=1)
        z, sm, acp = _inproj_local(
            xf, S, row(norm_mix[l]), w_local, w_rest, w_small, conv_a_w[l].astype(F32),
            row(conv_a_b[l]), row(ln_a_g[l]), row(ln_a_b[l]), row(gm_ln_g[l]), row(gm_ln_b[l]),
            gm_ws[l].astype(F32), bs_full, pool_w[l].astype(BF16), row(pool_scale[l]))
        dn = _deltanet(z, sm, B, S, dn_conv_w[l].astype(F32), _pad_lanes(dn_a_log[l]),
                       _pad_lanes(dn_dt_bias[l]), row(dn_norm_g[l]))
        xf = _merge(xf, acp, dn, z, w_branch[l].astype(BF16), w_out[l].astype(BF16))

        kv = _rms_matmul(memf, row(norm_mem[l]), xa_wkv[l].astype(BF16), 1024)
        xf = _xattn(xf, row(norm_xa[l]), xa_wq[l].astype(BF16), kv, xa_wo[l].astype(BF16), S)

        xf = _mlp(xf, row(norm_mlp[l]), mlp_w1[l].astype(BF16), mlp_w2[l].astype(BF16),
                  row(norm_f), final_norm=(l == depth - 1))
    return xf.reshape(B, S, D).astype(x.dtype)
```

```python
import functools

import jax
import jax.numpy as jnp
import numpy as np
from jax import lax
from jax.experimental import pallas as pl
from jax.experimental.pallas import tpu as pltpu

F32 = jnp.float32
BF16 = jnp.bfloat16

D_MODEL = 1024
MIX_W = D_MODEL // 2
N_BRANCH = 4
CONV_K = 31
DN_HEADS = 4
DN_HEAD_DIM = MIX_W // DN_HEADS
DN_CONV_K = 4
DN_CHUNK = 64
GM_GROUPS = 4
GM_CHUNK = 128
POOL_WINDOWS = (2, 4, 8, 16)
XA_HEADS = 4
XA_HEAD_DIM = D_MODEL // XA_HEADS
FFN_W = 4 * D_MODEL

V7X_VMEM_BYTES = 64 * 1024 * 1024
LANES = 128
SUBLANES = 8

ZL_W = 2 * MIX_W + 2 * MIX_W + MIX_W
ZR_W = 4 * MIX_W + N_BRANCH * D_MODEL
SMALL_W = 2 * LANES
IN_BM = 1024
IN_STEPS = 4
ZR_BN = ZR_W // IN_STEPS
ZL_DOT = 512
IN_DOT = 512
LOCAL_PIECES_PER_DOT = 8

CONV_HALO = 32
POOL_HALO = 32
DN_HALO = 8
LOCAL_TS = IN_BM // IN_STEPS
DN_TS = 256
DN_DIAG = 128


def _vmem_limit(block_bytes, scratch_bytes=0, temp_bytes=0):
    need = 2 * block_bytes + scratch_bytes + temp_bytes + (4 << 20)
    return int(min(need, V7X_VMEM_BYTES - (6 << 20)))


def _nbytes(shape, dtype):
    return int(np.prod(shape)) * jnp.dtype(dtype).itemsize


def _rmsnorm(x, g, eps=1e-6):
    return x * lax.rsqrt(jnp.mean(x * x, axis=-1, keepdims=True) + eps) * g


def _layernorm(x, g, b, eps=1e-5):
    mu = jnp.mean(x, axis=-1, keepdims=True)
    xc = x - mu
    var = jnp.mean(xc * xc, axis=-1, keepdims=True)
    return xc * lax.rsqrt(var + eps) * g + b


def _sigmoid(x):
    return 0.5 * jnp.tanh(0.5 * x) + 0.5


def _swish(x):
    h = 0.5 * x
    return h * jnp.tanh(h) + h


def _dot(a, b):
    return jnp.dot(a, b, preferred_element_type=F32)


def _dot_nt(a, b):
    return lax.dot_general(a, b, (((1,), (1,)), ((), ())), preferred_element_type=F32)


def _dot_tn(a, b):
    return lax.dot_general(a, b, (((0,), (0,)), ((), ())), preferred_element_type=F32)


def _interleave(*weighted):
    live = list(weighted)
    while live:
        for item in list(live):
            gen, n = item
            for _ in range(n):
                if next(gen, StopIteration) is StopIteration:
                    live.remove(item)
                    break


def _proj_columns(h_ref, w_ref, z_ref, chunk):
    for c0 in range(0, w_ref.shape[1], chunk):
        z_ref[:, c0:c0 + chunk] = _dot(h_ref[...], w_ref[:, c0:c0 + chunk]).astype(z_ref.dtype)
        yield


def _inproj_local_kernel(x_ref, g_ref, wl_ref, wr_ref, ws_ref, cw_ref, cb_ref, lag_ref, lab_ref,
                         glg_ref, glb_ref, gws_ref, bs_ref, pw_ref, ps_ref,
                         zr_ref, sm_ref, acp_ref,
                         h_ref, zl_ref, zt_ref, abuf, ashift, cwb, ybuf, pbuf, tbuf, *, tiles_per_seq):
    i = pl.program_id(0)
    j = pl.program_id(1)

    @pl.when(j == 0)
    def _():
        h_ref[...] = _rmsnorm(x_ref[...], g_ref[...]).astype(BF16)
        sm_ref[...] = _dot(h_ref[...], ws_ref[...])
        for _ in _proj_columns(h_ref, wl_ref, zl_ref, ZL_DOT):
            pass

    @pl.when(j > 0)
    def _():
        sub = j - 1
        tile = (i % tiles_per_seq) * IN_STEPS + sub
        zt_ref[...] = zl_ref[pl.ds(pl.multiple_of(sub * LOCAL_TS, LOCAL_TS), LOCAL_TS), :]
        w = MIX_W
        local = _local_tile(
            zt_ref.at[:, pl.ds(0, 2 * w)], zt_ref.at[:, pl.ds(2 * w, 2 * w)],
            zt_ref.at[:, pl.ds(4 * w, w)], cw_ref, cb_ref, lag_ref, lab_ref, glg_ref, glb_ref,
            gws_ref, bs_ref, pw_ref, ps_ref, acp_ref, abuf, ashift, cwb, ybuf, pbuf, tbuf, tile)
        next(local)
        _interleave((_proj_columns(h_ref, wr_ref, zr_ref, IN_DOT), 1), (local, LOCAL_PIECES_PER_DOT))


def _inproj_local(x, S, layer, g, wl, wr, ws, cw, cb, lag, lab, glg, glb, gws, bs_full, pw, ps):
    T, D = x.shape
    bm = IN_BM
    assert S % bm == 0 and T % bm == 0
    w = MIX_W
    ts = LOCAL_TS
    const = lambda *shape: pl.BlockSpec(shape, lambda i, j: (0,) * len(shape))
    prev = lambda j: jnp.maximum(j - 1, 0)
    blocks = (_nbytes((bm, D), F32) + _nbytes((D, ZR_BN), BF16) + _nbytes((D, SMALL_W), BF16)
              + _nbytes((bm, ZR_BN), BF16) + _nbytes((bm, SMALL_W), F32) + _nbytes((ts, 3 * w), BF16)
              + _nbytes((CONV_K + 8, w), F32) + _nbytes((GM_GROUPS, GM_CHUNK, GM_CHUNK), F32)
              + _nbytes((GM_CHUNK, w), F32) + _nbytes((4, 128, 128), BF16))
    scratch = (_nbytes((D, ZL_W), BF16) + _nbytes((bm, D), BF16) + _nbytes((bm, ZL_W), BF16)
               + SUBLANES * _nbytes((CONV_HALO + ts, w), F32) + _nbytes((ts, w), F32)
               + _nbytes((CONV_K + 1, SUBLANES, w), F32) + 2 * _nbytes((POOL_HALO + ts, w), F32))
    return pl.pallas_call(
        functools.partial(_inproj_local_kernel, tiles_per_seq=S // bm),
        grid=(T // bm, 1 + IN_STEPS),
        in_specs=[
            pl.BlockSpec((bm, D), lambda i, j: (i, 0)),
            const(1, D),
            pl.BlockSpec((None, D, ZL_W), lambda i, j: (layer, 0, 0), pipeline_mode=pl.Buffered(1)),
            pl.BlockSpec((None, D, ZR_BN), lambda i, j: (layer, 0, prev(j))),
            pl.BlockSpec((None, D, SMALL_W), lambda i, j: (layer, 0, 0)),
            const(CONV_K, w), const(1, w), const(1, w), const(1, w), const(1, w), const(1, w),
            const(GM_GROUPS, GM_CHUNK, GM_CHUNK), const(GM_CHUNK, w),
            const(len(POOL_WINDOWS), w // 4, w // 4), const(1, w),
        ],
        out_specs=[
            pl.BlockSpec((bm, ZR_BN), lambda i, j: (i, prev(j))),
            pl.BlockSpec((bm, SMALL_W), lambda i, j: (i, 0)),
            pl.BlockSpec((ts, 3 * w), lambda i, j: (i * IN_STEPS + prev(j), 0)),
        ],
        out_shape=[jax.ShapeDtypeStruct((T, ZR_W), BF16), jax.ShapeDtypeStruct((T, SMALL_W), F32),
                   jax.ShapeDtypeStruct((T, 3 * w), BF16)],
        scratch_shapes=[
            pltpu.VMEM((bm, D), BF16),
            pltpu.VMEM((bm, ZL_W), BF16),
            pltpu.VMEM((ts, ZL_W), BF16),
            pltpu.VMEM((CONV_HALO + ts, w), F32),
            pltpu.VMEM((SUBLANES - 1, CONV_HALO + ts - SUBLANES, w), F32),
            pltpu.VMEM((CONV_K + 1, SUBLANES, w), F32),
            pltpu.VMEM((ts, w), F32),
            pltpu.VMEM((POOL_HALO + ts, w), F32),
            pltpu.VMEM((POOL_HALO + ts, w), F32),
        ],
        compiler_params=pltpu.CompilerParams(
            dimension_semantics=("arbitrary", "arbitrary"),
            vmem_limit_bytes=_vmem_limit(blocks, scratch,
                                         _nbytes((bm, D), F32) + _nbytes((bm, max(ZL_DOT, IN_DOT)), F32)
                                         + 8 * _nbytes((ts, 2 * w), F32))),
        name="inproj_local",
    )(x, g, wl, wr, ws, cw, cb, lag, lab, glg, glb, gws, bs_full, pw, ps)


def _rms_matmul_kernel(x_ref, g_ref, w_ref, o_ref, h_ref):
    @pl.when(pl.program_id(1) == 0)
    def _():
        h_ref[...] = _rmsnorm(x_ref[...], g_ref[...]).astype(BF16)

    o_ref[...] = _dot(h_ref[...], w_ref[...]).astype(o_ref.dtype)


def _rms_matmul(x, g, w, layer, bn):
    T, D = x.shape
    N = w.shape[2]
    bm = min(1024, T)
    blocks = _nbytes((bm, D), F32) + _nbytes((D, bn), BF16) + _nbytes((bm, bn), BF16)
    return pl.pallas_call(
        _rms_matmul_kernel,
        grid=(T // bm, N // bn),
        in_specs=[
            pl.BlockSpec((bm, D), lambda i, j: (i, 0)),
            pl.BlockSpec((1, D), lambda i, j: (0, 0)),
            pl.BlockSpec((None, D, bn), lambda i, j: (layer, 0, j)),
        ],
        out_specs=pl.BlockSpec((bm, bn), lambda i, j: (i, j)),
        out_shape=jax.ShapeDtypeStruct((T, N), BF16),
        scratch_shapes=[pltpu.VMEM((bm, D), BF16)],
        compiler_params=pltpu.CompilerParams(
            dimension_semantics=("parallel", "arbitrary"),
            vmem_limit_bytes=_vmem_limit(blocks, _nbytes((bm, D), BF16), _nbytes((bm, D), F32))),
        name="mem_kv",
    )(x, g, w)


def _local_tile(ain_ref, gm_ref, pin_ref, cw_ref, cb_ref, lag_ref, lab_ref, glg_ref, glb_ref,
                ws_ref, bs_ref, pw_ref, ps_ref, out_ref, abuf, ashift, cwb, ybuf, pbuf, tbuf, tile):
    ts = ain_ref.shape[0]
    w = MIX_W

    @pl.when(tile == 0)
    def _():
        abuf[0:CONV_HALO, :] = jnp.zeros((CONV_HALO, w), F32)
        pbuf[0:POOL_HALO, :] = jnp.zeros((POOL_HALO, w), F32)
        for k in range(CONV_K):
            cwb[k] = jnp.broadcast_to(cw_ref[k:k + 1, :], (SUBLANES, w))
        cwb[CONV_K] = jnp.broadcast_to(cb_ref[...], (SUBLANES, w))
    yield

    ain = ain_ref[...].astype(F32)
    abuf[CONV_HALO:CONV_HALO + ts, :] = ain[:, :w] * _sigmoid(ain[:, w:])
    yield
    n_sh = CONV_HALO + ts - SUBLANES
    for r in range(1, SUBLANES):
        ashift[r - 1] = abuf[r:r + n_sh, :]
        if r % 2 == 0:
            yield
    rb = 4 * SUBLANES
    for r0 in range(0, ts, rb):
        acc = [cwb[CONV_K]] * (rb // SUBLANES)
        for k in range(CONV_K):
            off = CONV_HALO - (CONV_K - 1) + k + r0
            r = off % SUBLANES
            wk = cwb[k]
            for q in range(rb // SUBLANES):
                lo = off - r + q * SUBLANES
                src = abuf[lo:lo + SUBLANES, :] if r == 0 else ashift[r - 1, lo:lo + SUBLANES, :]
                acc[q] = acc[q] + wk * src
        for q in range(rb // SUBLANES):
            ybuf[r0 + q * SUBLANES:r0 + (q + 1) * SUBLANES, :] = acc[q]
        yield
    abuf[0:CONV_HALO, :] = abuf[ts:ts + CONV_HALO, :]
    ya = _layernorm(ybuf[...], lag_ref[...], lab_ref[...])
    out_ref[:, 0:w] = _swish(ya).astype(out_ref.dtype)
    yield

    gm = gm_ref[...].astype(F32)
    ge = 0.5 * gm * (1.0 + lax.erf(gm * np.float32(np.sqrt(0.5))))
    yield
    ybuf[...] = _layernorm(ge[:, w:], glg_ref[...], glb_ref[...])
    yield
    gw = w // GM_GROUPS
    row = lax.broadcasted_iota(jnp.int32, (GM_CHUNK, GM_CHUNK), 0)
    col = lax.broadcasted_iota(jnp.int32, (GM_CHUNK, GM_CHUNK), 1)
    for g in range(GM_GROUPS):
        wsg = jnp.where(row >= col, ws_ref[g], 0.0).astype(BF16)
        for n in range(ts // GM_CHUNK):
            rows = slice(n * GM_CHUNK, (n + 1) * GM_CHUNK)
            cols = slice(g * gw, (g + 1) * gw)
            mixed = _dot(wsg, ybuf[rows, cols].astype(BF16)) + bs_ref[:, cols]
            out_ref[rows, w + g * gw:w + (g + 1) * gw] = (ge[rows, cols] * mixed).astype(out_ref.dtype)
        yield

    pin = pin_ref[...].astype(F32)
    pbuf[POOL_HALO:POOL_HALO + ts, :] = pin
    tpos = tile * ts + lax.broadcasted_iota(jnp.int32, (ts, 1), 0)
    n_ext = POOL_HALO + ts
    pw_cols = w // len(POOL_WINDOWS)
    for gi, win in enumerate(POOL_WINDOWS):
        cols = slice(gi * pw_cols, (gi + 1) * pw_cols)
        src = pbuf
        d = 1
        lo = SUBLANES
        while d < win:
            tbuf[lo:n_ext, cols] = src[lo:n_ext, cols] + src[lo - d:n_ext - d, cols]
            src = tbuf
            d *= 2
            lo += SUBLANES
        wsum = src[POOL_HALO:n_ext, cols]
        inv = 1.0 / jnp.minimum(tpos + 1, win).astype(F32)
        pg = (wsum * inv - pin[:, cols]).astype(BF16)
        lin = _dot(pg, pw_ref[gi]) * ps_ref[:, cols]
        out_ref[:, 2 * w + gi * pw_cols:2 * w + (gi + 1) * pw_cols] = lin.astype(out_ref.dtype)
        yield
    pbuf[0:POOL_HALO, :] = pbuf[ts:ts + POOL_HALO, :]


def _split3(x):
    hi = x.astype(BF16)
    r1 = x - hi.astype(F32)
    mid = r1.astype(BF16)
    lo = (r1 - mid.astype(F32)).astype(BF16)
    return hi, mid, lo


def _dot_mask(mask_bf16, x):
    hi, mid, lo = _split3(x)
    return _dot(mask_bf16, hi) + _dot(mask_bf16, mid) + _dot(mask_bf16, lo)


def _deltanet_masks(ts):
    def causal_same(n):
        row, col = np.indices((n, n))
        same = (row // DN_CHUNK) == (col // DN_CHUNK)
        return same & (row >= col), same, row, col

    causal, same, _, _ = causal_same(ts)
    mk = np.stack([causal, same]).astype(np.float32)
    causal, _, row, col = causal_same(DN_DIAG)
    levels = []
    m = 1
    while m < DN_CHUNK:
        levels.append(((row // (2 * m)) == (col // (2 * m))) & ((row & m) != 0) & ((col & m) == 0))
        m *= 2
    mf = np.stack([causal] + levels + [row == col]).astype(np.float32)
    return jnp.asarray(mk, BF16), jnp.asarray(mf, F32)


def _blockdiag_bf16(squares):
    n = len(squares)
    zero = jnp.zeros(squares[0].shape, BF16)
    rows = [jnp.concatenate([squares[i].astype(BF16) if j == i else zero for j in range(n)], axis=1)
            for i in range(n)]
    return jnp.concatenate(rows, axis=0)


def _deltanet_front(qk_ref, vd_ref, sm_ref, cw_ref, alog_ref, dtb_ref, mk_ref, mf_ref, xbuf, qkv_s,
                    stage):
    a_s, attn_s, rhs_s, qdec_s, kdec_s, egl_s, gate_s = stage
    ts = qk_ref.shape[0]
    w = MIX_W
    dh = DN_HEAD_DIM

    for cg in range(3 * w // LANES):
        cols = slice(cg * LANES, (cg + 1) * LANES)
        src = qk_ref[:, cols] if cg < 2 * w // LANES else vd_ref[:, cg * LANES - 2 * w:(cg + 1) * LANES - 2 * w]
        xbuf[DN_HALO:DN_HALO + ts, cols] = src.astype(F32)
        y = None
        for k in range(DN_CONV_K):
            off = DN_HALO - (DN_CONV_K - 1) + k
            tap = cw_ref[k:k + 1, cols] * xbuf[off:off + ts, cols]
            y = tap if y is None else y + tap
        xbuf[0:DN_HALO, cols] = xbuf[ts:ts + DN_HALO, cols]
        qkv_s[:, cols] = _swish(y)
        if cg % 2 == 1:
            yield

    sm = sm_ref[...]
    beta = _sigmoid(sm[:, 0:LANES])
    da = sm[:, LANES:2 * LANES] + dtb_ref[...]
    softplus = jnp.maximum(da, 0.0) + jnp.log1p(jnp.exp(-jnp.abs(da)))
    g = -jnp.exp(alog_ref[...]) * softplus

    gam = _dot_mask(mk_ref[0], g)
    glast = _dot_mask(mk_ref[1], g)
    gam_t = gam.T
    causal = mf_ref[0]
    egl_s[...] = jnp.exp(glast)
    yield
    for h in range(DN_HEADS):
        dg = vd_ref[:, w + h * dh:w + (h + 1) * dh].astype(F32)
        gate_s[:, h * dh:(h + 1) * dh] = _swish(dg)
        qh = qkv_s[:, h * dh:(h + 1) * dh]
        kh = qkv_s[:, w + h * dh:w + (h + 1) * dh]
        vh = qkv_s[:, 2 * w + h * dh:2 * w + (h + 1) * dh]
        qn = qh * lax.rsqrt(jnp.sum(qh * qh, axis=-1, keepdims=True) + 1e-6) * np.float32(dh ** -0.5)
        kn = kh * lax.rsqrt(jnp.sum(kh * kh, axis=-1, keepdims=True) + 1e-6)
        bcol = beta[:, h:h + 1]
        gcol = gam[:, h:h + 1]
        kb = kn * bcol
        kn16 = kn.astype(BF16)
        kk = _dot_nt(kb.astype(BF16), kn16)
        qk = _dot_nt(qn.astype(BF16), kn16)
        for d in range(ts // DN_DIAG):
            dd = slice(d * DN_DIAG, (d + 1) * DN_DIAG)
            decay = jnp.exp((gcol[dd] - gam_t[h:h + 1, dd]) * causal) * causal
            a_s[h, d] = kk[dd, dd] * decay
            attn_s[h, dd, dd] = (qk[dd, dd] * decay).astype(BF16)
        eg = jnp.exp(gcol)
        rhs_s[h] = jnp.concatenate([vh * bcol, kb * eg], axis=1).astype(BF16)
        qdec_s[h] = (qn * eg).astype(BF16)
        kdec_s[h] = (kn * jnp.exp(glast[:, h:h + 1] - gcol)).astype(BF16)
        yield


def _deltanet_back(ng_ref, mf_ref, o_ref, stage, state, sol_s, vnew_s, qs_s):
    a_s, attn_s, rhs_s, qdec_s, kdec_s, egl_s, gate_s = stage
    ts = o_ref.shape[0]
    dh = DN_HEAD_DIM
    ck = DN_CHUNK
    heads = range(DN_HEADS)

    n_lvl = ck.bit_length() - 1
    diags = range(ts // DN_DIAG)
    sq = lambda m, d: m[d * DN_DIAG:(d + 1) * DN_DIAG, d * DN_DIAG:(d + 1) * DN_DIAG]
    x = [[mf_ref[1 + n_lvl] - mf_ref[1] * a_s[h, d] for d in diags] for h in heads]
    for lvl in range(1, n_lvl):
        x16 = [_blockdiag_bf16(x[h]) for h in heads]
        b16 = [_blockdiag_bf16([mf_ref[1 + lvl] * a_s[h, d] for d in diags]) for h in heads]
        y = [_dot(b16[h], x16[h]) for h in heads]
        y16 = [_blockdiag_bf16([sq(y[h], d) for d in diags]) for h in heads]
        xy = [_dot(x16[h], y16[h]) for h in heads]
        x = [[x[h][d] - sq(xy[h], d) for d in diags] for h in heads]
        yield
    for h in heads:
        sol_s[h] = _dot(_blockdiag_bf16(x[h]), rhs_s[h])
    yield

    st = [state[h] for h in heads]
    for c in range(ts // ck):
        rows = slice(c * ck, (c + 1) * ck)
        for h in heads:
            wq = jnp.concatenate([sol_s[h, rows, dh:2 * dh].astype(BF16), qdec_s[h, rows, :]], axis=0)
            r = _dot(wq, st[h].astype(BF16))
            vnew = sol_s[h, rows, 0:dh] - r[0:ck, :]
            vnew_s[h, rows, :] = vnew.astype(BF16)
            qs_s[h, rows, :] = r[ck:2 * ck, :]
            scale = egl_s[c * ck:c * ck + 1, h:h + 1]
            st[h] = st[h] * scale + _dot_tn(kdec_s[h, rows, :], vnew.astype(BF16))
        yield
    for h in heads:
        state[h] = st[h]
        o = qs_s[h] + _dot(attn_s[h], vnew_s[h])
        on = _rmsnorm(o, ng_ref[...])
        o_ref[:, h * dh:(h + 1) * dh] = (on * gate_s[:, h * dh:(h + 1) * dh]).astype(o_ref.dtype)


def _deltanet_kernel(qk_ref, vd_ref, sm_ref, cw_ref, alog_ref, dtb_ref, ng_ref, mk_ref, mf_ref,
                     o_ref, xbuf, qkv_s, state, sol_s, vnew_s, qs_s, *stages):
    s = pl.program_id(1)
    n_stage = len(stages) // 2
    stage_a, stage_b = stages[:n_stage], stages[n_stage:]

    @pl.when(s == 0)
    def _():
        xbuf[0:DN_HALO, :] = jnp.zeros((DN_HALO, xbuf.shape[1]), F32)
        state[...] = jnp.zeros(state.shape, F32)
        for ref in stage_a + stage_b:
            ref[...] = jnp.zeros(ref.shape, ref.dtype)

    def step(write_stage, read_stage):
        halves = [
            _deltanet_back(ng_ref, mf_ref, o_ref, read_stage, state, sol_s, vnew_s, qs_s),
            _deltanet_front(qk_ref, vd_ref, sm_ref, cw_ref, alog_ref, dtb_ref, mk_ref, mf_ref, xbuf,
                            qkv_s, write_stage),
        ]
        while halves:
            for half in list(halves):
                if next(half, StopIteration) is StopIteration:
                    halves.remove(half)

    @pl.when(s % 2 == 0)
    def _():
        step(stage_a, stage_b)

    @pl.when(s % 2 == 1)
    def _():
        step(stage_b, stage_a)


def _deltanet(z, sm, B, S, cw, alog, dtb, ng):
    ts = min(DN_TS, S)
    ns = S // ts
    w = MIX_W
    dh = DN_HEAD_DIM
    const = lambda *shape: pl.BlockSpec(shape, lambda b, s: (0,) * len(shape))
    mk, mf = _deltanet_masks(ts)
    in_row = lambda b, s: b * ns + jnp.minimum(s, ns - 1)
    out_row = lambda b, s: b * ns + jnp.maximum(s - 1, 0)
    stage = [
        pltpu.VMEM((DN_HEADS, ts // DN_DIAG, DN_DIAG, DN_DIAG), F32),
        pltpu.VMEM((DN_HEADS, ts, ts), BF16),
        pltpu.VMEM((DN_HEADS, ts, 2 * dh), BF16),
        pltpu.VMEM((DN_HEADS, ts, dh), BF16),
        pltpu.VMEM((DN_HEADS, ts, dh), BF16),
        pltpu.VMEM((ts, LANES), F32),
        pltpu.VMEM((ts, w), F32),
    ]
    stage_bytes = (DN_HEADS * (_nbytes((ts, ts), F32) + _nbytes((ts, ts), BF16) + 4 * _nbytes((ts, dh), BF16))
                   + _nbytes((ts, LANES), F32) + _nbytes((ts, w), F32))
    blocks = (2 * _nbytes((ts, 2 * w), BF16) + _nbytes((ts, SMALL_W), F32) + _nbytes((8, 3 * w), F32)
              + _nbytes((ts, w), BF16) + _nbytes(mk.shape, BF16) + _nbytes(mf.shape, F32))
    scratch = (2 * _nbytes((DN_HALO + ts, 3 * w), F32) + _nbytes((DN_HEADS, dh, dh), F32)
               + DN_HEADS * _nbytes((ts, 4 * dh), F32) + 2 * stage_bytes)
    return pl.pallas_call(
        _deltanet_kernel,
        grid=(B, ns + 1),
        in_specs=[
            pl.BlockSpec((ts, 2 * w), lambda b, s: (in_row(b, s), 0)),
            pl.BlockSpec((ts, 2 * w), lambda b, s: (in_row(b, s), 1)),
            pl.BlockSpec((ts, SMALL_W), lambda b, s: (in_row(b, s), 0)),
            const(DN_CONV_K, 3 * w), const(1, LANES), const(1, LANES), const(1, dh),
            const(*mk.shape), const(*mf.shape),
        ],
        out_specs=pl.BlockSpec((ts, w), lambda b, s: (out_row(b, s), 0)),
        out_shape=jax.ShapeDtypeStruct((B * S, w), BF16),
        scratch_shapes=[
            pltpu.VMEM((DN_HALO + ts, 3 * w), F32),
            pltpu.VMEM((ts, 3 * w), F32),
            pltpu.VMEM((DN_HEADS, dh, dh), F32),
            pltpu.VMEM((DN_HEADS, ts, 2 * dh), F32),
            pltpu.VMEM((DN_HEADS, ts, dh), BF16),
            pltpu.VMEM((DN_HEADS, ts, dh), F32),
        ] + stage + stage,
        compiler_params=pltpu.CompilerParams(
            dimension_semantics=("parallel", "arbitrary"),
            vmem_limit_bytes=_vmem_limit(blocks, scratch, 24 * _nbytes((ts, ts), F32)
                                         + 4 * _nbytes((ts, 3 * w), F32))),
        name="deltanet",
    )(z, z, sm, cw, alog, dtb, ng, mk, mf)


def _merge_kernel(x_ref, acp_ref, dn_ref, g0_ref, g1_ref, g2_ref, g3_ref, wb_ref, wo_ref, out_ref):
    w = MIX_W
    branches = (acp_ref[:, 0:w], dn_ref[...], acp_ref[:, w:2 * w], acp_ref[:, 2 * w:3 * w])
    gates = (g0_ref, g1_ref, g2_ref, g3_ref)
    merged = None
    for n in range(N_BRANCH):
        term = _sigmoid(gates[n][...].astype(F32)) * _dot(branches[n], wb_ref[n])
        merged = term if merged is None else merged + term
    out_ref[...] = x_ref[...] + _dot(merged.astype(BF16), wo_ref[...])


def _merge(x, acp, dn, z, wb, wo, layer):
    T, D = x.shape
    w = MIX_W
    bm = min(512, T)
    gate_blk0 = (4 * w) // D
    gate_spec = lambda n: pl.BlockSpec((bm, D), lambda i: (i, gate_blk0 + n))
    blocks = (2 * _nbytes((bm, D), F32) + _nbytes((bm, 4 * w), BF16) + 4 * _nbytes((bm, D), BF16)
              + _nbytes((N_BRANCH, w, D), BF16) + _nbytes((D, D), BF16))
    return pl.pallas_call(
        _merge_kernel,
        grid=(T // bm,),
        in_specs=[
            pl.BlockSpec((bm, D), lambda i: (i, 0)),
            pl.BlockSpec((bm, 3 * w), lambda i: (i, 0)),
            pl.BlockSpec((bm, w), lambda i: (i, 0)),
            gate_spec(0), gate_spec(1), gate_spec(2), gate_spec(3),
            pl.BlockSpec((None, N_BRANCH, w, D), lambda i: (layer, 0, 0, 0)),
            pl.BlockSpec((None, D, D), lambda i: (layer, 0, 0)),
        ],
        out_specs=pl.BlockSpec((bm, D), lambda i: (i, 0)),
        out_shape=jax.ShapeDtypeStruct((T, D), F32),
        compiler_params=pltpu.CompilerParams(
            dimension_semantics=("parallel",),
            vmem_limit_bytes=_vmem_limit(blocks, 0, 4 * _nbytes((bm, D), F32))),
        name="merge",
    )(x, acp, dn, z, z, z, z, wb, wo)


def _xattn_kernel(x_ref, g_ref, wq_ref, kv_ref, wo_ref, out_ref):
    x = x_ref[...]
    hn = _rmsnorm(x, g_ref[...]).astype(BF16)
    hd = XA_HEAD_DIM
    heads = range(XA_HEADS)
    q = _dot(hn, wq_ref[...])
    sc, p, o = {}, {}, {}

    def stage(h, t):
        cols = slice(h * hd, (h + 1) * hd)
        if t == 0:
            sc[h] = _dot_nt(q[:, cols].astype(BF16), kv_ref[:, cols]) * np.float32(hd ** -0.5)
        elif t == 1:
            e = jnp.exp(sc[h] - jnp.max(sc[h], axis=-1, keepdims=True))
            p[h] = (e / jnp.sum(e, axis=-1, keepdims=True)).astype(BF16)
        else:
            o[h] = _dot(p[h], kv_ref[:, D_MODEL + h * hd:D_MODEL + (h + 1) * hd]).astype(BF16)

    n_stage = 3
    for t in range(n_stage + XA_HEADS - 1):
        for h in heads:
            if 0 <= t - h < n_stage:
                stage(h, t - h)
    out_ref[...] = x + _dot(jnp.concatenate([o[h] for h in heads], axis=1), wo_ref[...])


def _xattn(x, g, wq, kv, wo, layer, S):
    T, D = x.shape
    M = kv.shape[0] // (T // S)
    bm = min(1024, S)
    per_b = S // bm
    blocks = (2 * _nbytes((bm, D), F32) + 2 * _nbytes((D, D), BF16) + _nbytes((M, 2 * D), BF16))
    return pl.pallas_call(
        _xattn_kernel,
        grid=(T // bm,),
        in_specs=[
            pl.BlockSpec((bm, D), lambda i: (i, 0)),
            pl.BlockSpec((1, D), lambda i: (0, 0)),
            pl.BlockSpec((None, D, D), lambda i: (layer, 0, 0)),
            pl.BlockSpec((M, 2 * D), lambda i: (i // per_b, 0)),
            pl.BlockSpec((None, D, D), lambda i: (layer, 0, 0)),
        ],
        out_specs=pl.BlockSpec((bm, D), lambda i: (i, 0)),
        out_shape=jax.ShapeDtypeStruct((T, D), F32),
        compiler_params=pltpu.CompilerParams(
            dimension_semantics=("parallel",),
            vmem_limit_bytes=_vmem_limit(blocks, 0, 6 * _nbytes((bm, D), F32))),
        name="xattn",
    )(x, g, wq, kv, wo)


def _mlp_kernel(x_ref, g_ref, w1_ref, w2_ref, gf_ref, out_ref, h_ref, *, final_norm):
    f = pl.program_id(1)

    @pl.when(f == 0)
    def _():
        x = x_ref[...]
        h_ref[...] = _rmsnorm(x, g_ref[...]).astype(BF16)
        out_ref[...] = x

    t = jnp.maximum(_dot(h_ref[...], w1_ref[...]), 0.0)
    out_ref[...] += _dot((t * t).astype(BF16), w2_ref[...])

    if final_norm:
        @pl.when(f == pl.num_programs(1) - 1)
        def _():
            out_ref[...] = _rmsnorm(out_ref[...], gf_ref[...])


def _mlp(x, g, w1, w2, layer, gf, final_norm):
    T, D = x.shape
    F = w1.shape[2]
    bm = min(1024, T)
    bf = 1024
    blocks = (2 * _nbytes((bm, D), F32) + 2 * _nbytes((D, bf), BF16))
    return pl.pallas_call(
        functools.partial(_mlp_kernel, final_norm=final_norm),
        grid=(T // bm, F // bf),
        in_specs=[
            pl.BlockSpec((bm, D), lambda i, f: (i, 0)),
            pl.BlockSpec((1, D), lambda i, f: (0, 0)),
            pl.BlockSpec((None, D, bf), lambda i, f: (layer, 0, f)),
            pl.BlockSpec((None, bf, D), lambda i, f: (layer, f, 0)),
            pl.BlockSpec((1, D), lambda i, f: (0, 0)),
        ],
        out_specs=pl.BlockSpec((bm, D), lambda i, f: (i, 0)),
        out_shape=jax.ShapeDtypeStruct((T, D), F32),
        scratch_shapes=[pltpu.VMEM((bm, D), BF16)],
        compiler_params=pltpu.CompilerParams(
            dimension_semantics=("parallel", "arbitrary"),
            vmem_limit_bytes=_vmem_limit(blocks, _nbytes((bm, D), BF16), 3 * _nbytes((bm, bf), F32))),
        name="mlp",
    )(x, g, w1, w2, gf)


def _permute_w_in(w_in):
    w = MIX_W
    sizes = [2 * w, w, w, w, w, DN_HEADS, DN_HEADS, 2 * w, w, N_BRANCH * D_MODEL]
    idx = np.cumsum([0] + sizes)
    seg = lambda i: w_in[..., idx[i]:idx[i + 1]]
    a_in, dq, dk, dv, dgate, dbeta, da, gm_in, pool_in, gate_in = (seg(i) for i in range(10))
    local = jnp.concatenate([a_in, gm_in, pool_in], axis=-1).astype(BF16)
    rest = jnp.concatenate([dq, dk, dv, dgate, gate_in], axis=-1).astype(BF16)
    pad = jnp.zeros(w_in.shape[:-1] + (LANES - DN_HEADS,), w_in.dtype)
    small = jnp.concatenate([dbeta, pad, da, pad], axis=-1).astype(BF16)
    return local, rest, small


def _pad_lanes(v):
    return jnp.pad(v.astype(F32), (0, LANES - v.shape[0])).reshape(1, LANES)


def kernel(x, mem, norm_mix, w_in, conv_a_w, conv_a_b, ln_a_g, ln_a_b, dn_conv_w, dn_a_log, dn_dt_bias, dn_norm_g, gm_ln_g, gm_ln_b, gm_ws, gm_bs, pool_w, pool_scale, w_branch, w_out, norm_xa, norm_mem, xa_wq, xa_wkv, xa_wo, norm_mlp, mlp_w1, mlp_w2, norm_f):
    B, S, D = x.shape
    depth = w_in.shape[0]
    assert D == D_MODEL and S % DN_CHUNK == 0 and S % GM_CHUNK == 0
    T = B * S
    xf = x.reshape(T, D).astype(F32)
    memf = mem.reshape(B * mem.shape[1], D).astype(F32)
    row = lambda v: v.astype(F32).reshape(1, -1)

    w_local, w_rest, w_small = _permute_w_in(w_in)
    wb16, wo16 = w_branch.astype(BF16), w_out.astype(BF16)
    wq16, wkv16, xwo16 = xa_wq.astype(BF16), xa_wkv.astype(BF16), xa_wo.astype(BF16)
    w1_16, w2_16 = mlp_w1.astype(BF16), mlp_w2.astype(BF16)

    for l in range(depth):
        bs_full = jnp.repeat(gm_bs[l].astype(F32).T, MIX_W // GM_GROUPS, axis=1)
        z, sm, acp = _inproj_local(
            xf, S, l, row(norm_mix[l]), w_local, w_rest, w_small, conv_a_w[l].astype(F32),
            row(conv_a_b[l]), row(ln_a_g[l]), row(ln_a_b[l]), row(gm_ln_g[l]), row(gm_ln_b[l]),
            gm_ws[l].astype(F32), bs_full, pool_w[l].astype(BF16), row(pool_scale[l]))
        dn = _deltanet(z, sm, B, S, dn_conv_w[l].astype(F32), _pad_lanes(dn_a_log[l]),
                       _pad_lanes(dn_dt_bias[l]), row(dn_norm_g[l]))
        xf = _merge(xf, acp, dn, z, wb16, wo16, l)

        kv = _rms_matmul(memf, row(norm_mem[l]), wkv16, l, 1024)
        xf = _xattn(xf, row(norm_xa[l]), wq16, kv, xwo16, l, S)

        xf = _mlp(xf, row(norm_mlp[l]), w1_16, w2_16, l, row(norm_f), final_norm=(l == depth - 1))
    return xf.reshape(B, S, D).astype(x.dtype)
```
